```python
import math
import jax, jax.numpy as jnp
from jax import lax
import numpy as np

D_MODEL = 1024
BATCH = 4
SEQ = 8192
DEPTH = 2

N_MIXERS = 2
N_HEADS = 8
HEAD_DIM = 64
V_DIM = 2 * HEAD_DIM
D_FF = 2816
CONV_WIDTH = 3
Q_BLOCK = 128
EPS = 1e-5
N_ATTN = (DEPTH + 1) // 2
N_CONV = DEPTH // 2

kernel_name = "hybrid_diffattn_shortconv_macaron"


def rmsnorm(x, g):
    xf = x.astype(jnp.float32)
    y = xf * lax.rsqrt(jnp.mean(xf * xf, axis=-1, keepdims=True) + EPS)
    return (y * g.astype(jnp.float32)).astype(x.dtype)


def swiglu(h, w_gate, w_up, w_down):
    return (jax.nn.silu(h @ w_gate) * (h @ w_up)) @ w_down


def diff_attention(h, w_qkv, lq1, lk1, lq2, lk2, subln, w_out, layer_idx):
    b, s, _ = h.shape
    qkv = h @ w_qkv
    q, k, v = jnp.split(qkv, 3, axis=-1)
    q = q.reshape(b, s, N_HEADS, 2, HEAD_DIM) * (HEAD_DIM ** -0.5)
    k = k.reshape(b, s, N_HEADS, 2, HEAD_DIM)
    v = v.reshape(b, s, N_HEADS, V_DIM)
    lambda_init = 0.8 - 0.6 * math.exp(-0.3 * layer_idx)
    lam = (jnp.exp(jnp.sum(lq1.astype(jnp.float32) * lk1.astype(jnp.float32)))
           - jnp.exp(jnp.sum(lq2.astype(jnp.float32) * lk2.astype(jnp.float32)))
           + lambda_init)
    nb = s // Q_BLOCK
    q_blocks = q.reshape(b, nb, Q_BLOCK, N_HEADS, 2, HEAD_DIM).transpose(1, 0, 2, 3, 4, 5)
    k_pos = jnp.arange(s)

    def one_block(args):
        q_blk, blk = args
        scores = jnp.einsum('bqhcd,bkhcd->bhcqk', q_blk, k).astype(jnp.float32)
        q_pos = blk * Q_BLOCK + jnp.arange(Q_BLOCK)
        causal = k_pos[None, :] <= q_pos[:, None]
        scores = jnp.where(causal, scores, -jnp.inf)
        p = jax.nn.softmax(scores, axis=-1)
        a = p[:, :, 0] - lam * p[:, :, 1]
        return jnp.einsum('bhqk,bkhe->bqhe', a.astype(v.dtype), v)

    o = lax.map(one_block, (q_blocks, jnp.arange(nb)))
    o = o.transpose(1, 0, 2, 3, 4).reshape(b, s, N_HEADS, V_DIM)
    o = rmsnorm(o, subln) * (1.0 - lambda_init)
    return o.reshape(b, s, N_HEADS * V_DIM) @ w_out


def short_conv(h, w_in, w_conv, w_out):
    s = h.shape[1]
    gb, gc, u = jnp.split(h @ w_in, 3, axis=-1)
    u = gc * u
    u_pad = jnp.pad(u, ((0, 0), (CONV_WIDTH - 1, 0), (0, 0)))
    y = sum(w_conv[j] * u_pad[:, j:j + s] for j in range(CONV_WIDTH))
    return (gb * y) @ w_out


def setup_inputs(seed: int = 0) -> dict:
    key = jax.random.key(seed)
    ks = jax.random.split(key, 24)
    D, F = D_MODEL, D_FF
    nrm = lambda k, shape, fan_in: jax.random.normal(k, shape, jnp.float32) * fan_in ** -0.5
    gain = lambda k, shape: 1.0 + 0.02 * jax.random.normal(k, shape, jnp.float32)
    return {
        "x": jax.random.normal(ks[0], (BATCH, SEQ, D), jnp.float32),
        "ffn1_norm": gain(ks[1], (DEPTH, D)),
        "ffn1_w_gate": nrm(ks[2], (DEPTH, D, F), D),
        "ffn1_w_up": nrm(ks[3], (DEPTH, D, F), D),
        "ffn1_w_down": nrm(ks[4], (DEPTH, F, D), F),
        "mix_norm": gain(ks[5], (DEPTH, D)),
        "attn_w_qkv": nrm(ks[6], (N_ATTN, D, 3 * D), D),
        "attn_lambda_q1": 0.1 * jax.random.normal(ks[7], (N_ATTN, HEAD_DIM), jnp.float32),
        "attn_lambda_k1": 0.1 * jax.random.normal(ks[8], (N_ATTN, HEAD_DIM), jnp.float32),
        "attn_lambda_q2": 0.1 * jax.random.normal(ks[9], (N_ATTN, HEAD_DIM), jnp.float32),
        "attn_lambda_k2": 0.1 * jax.random.normal(ks[10], (N_ATTN, HEAD_DIM), jnp.float32),
        "attn_subln": gain(ks[11], (N_ATTN, V_DIM)),
        "attn_w_out": nrm(ks[12], (N_ATTN, D, D), D),
        "conv_w_in": nrm(ks[13], (N_CONV, D, 3 * D), D),
        "conv_w": nrm(ks[14], (N_CONV, CONV_WIDTH, D), CONV_WIDTH),
        "conv_w_out": nrm(ks[15], (N_CONV, D, D), D),
        "ffn2_norm": gain(ks[16], (DEPTH, D)),
        "ffn2_w_gate": nrm(ks[17], (DEPTH, D, F), D),
        "ffn2_w_up": nrm(ks[18], (DEPTH, D, F), D),
        "ffn2_w_down": nrm(ks[19], (DEPTH, F, D), F),
        "final_norm": gain(ks[20], (D,)),
    }


def reference(x, ffn1_norm, ffn1_w_gate, ffn1_w_up, ffn1_w_down, mix_norm,
              attn_w_qkv, attn_lambda_q1, attn_lambda_k1, attn_lambda_q2, attn_lambda_k2,
              attn_subln, attn_w_out, conv_w_in, conv_w, conv_w_out,
              ffn2_norm, ffn2_w_gate, ffn2_w_up, ffn2_w_down, final_norm):
    for i in range(DEPTH):
        x = x + 0.5 * swiglu(rmsnorm(x, ffn1_norm[i]), ffn1_w_gate[i], ffn1_w_up[i], ffn1_w_down[i])
        h = rmsnorm(x, mix_norm[i])
        j = i // N_MIXERS
        if i % N_MIXERS == 0:
            x = x + diff_attention(h, attn_w_qkv[j], attn_lambda_q1[j], attn_lambda_k1[j],
                                   attn_lambda_q2[j], attn_lambda_k2[j], attn_subln[j],
                                   attn_w_out[j], i)
        else:
            x = x + short_conv(h, conv_w_in[j], conv_w[j], conv_w_out[j])
        x = x + 0.5 * swiglu(rmsnorm(x, ffn2_norm[i]), ffn2_w_gate[i], ffn2_w_up[i], ffn2_w_down[i])
    return rmsnorm(x, final_norm)
```

```python
import functools
import math

import jax
import jax.numpy as jnp
from jax import lax
from jax.experimental import pallas as pl
from jax.experimental.pallas import tpu as pltpu

D_MODEL = 1024
N_HEADS = 8
HEAD_DIM = 64
V_DIM = 2 * HEAD_DIM
D_FF = 2816
CONV_WIDTH = 3
EPS = 1e-5
N_MIXERS = 2

LANES = 128
SUBLANES = 8
MXU_DIM = 256
VMEM_LIMIT = 56 * 1024 * 1024

F32 = jnp.float32
BF16 = jnp.bfloat16


def _rms(x, g):
    ms = jnp.mean(x * x, axis=-1, keepdims=True)
    return x * lax.rsqrt(ms + EPS) * g


def _resident(shape):
    return pl.BlockSpec(shape, lambda *_: (0,) * len(shape),
                        pipeline_mode=pl.Buffered(1))


def _params(*sem):
    return pltpu.CompilerParams(dimension_semantics=sem,
                                vmem_limit_bytes=VMEM_LIMIT)


FFN_ROWS = 512
FFN_CHUNK = MXU_DIM


def _ffn_kernel(x_ref, g_ref, wg_ref, wu_ref, wd_ref, fin_ref, o_ref, h_ref, *,
                final_norm):
    x = x_ref[...]
    xn = _rms(x, g_ref[...]).astype(BF16)
    for c in range(D_FF // FFN_CHUNK):
        sl = slice(c * FFN_CHUNK, (c + 1) * FFN_CHUNK)
        gate = jnp.dot(xn, wg_ref[:, sl], preferred_element_type=F32)
        up = jnp.dot(xn, wu_ref[:, sl], preferred_element_type=F32)
        h_ref[:, sl] = (gate * jax.nn.sigmoid(gate) * up).astype(BF16)
    y = x + 0.5 * jnp.dot(h_ref[...], wd_ref[...], preferred_element_type=F32)
    if final_norm:
        y = _rms(y, fin_ref[...])
    o_ref[...] = y


def _ffn(x, g, wg, wu, wd, fin, final_norm):
    rows = x.shape[0]
    row_spec = pl.BlockSpec((FFN_ROWS, D_MODEL), lambda i: (i, 0))
    return pl.pallas_call(
        functools.partial(_ffn_kernel, final_norm=final_norm),
        grid=(rows // FFN_ROWS,),
        in_specs=[row_spec, _resident((1, D_MODEL)),
                  _resident((D_MODEL, D_FF)), _resident((D_MODEL, D_FF)),
                  _resident((D_FF, D_MODEL)), _resident((1, D_MODEL))],
        out_specs=row_spec,
        out_shape=jax.ShapeDtypeStruct((rows, D_MODEL), F32),
        scratch_shapes=[pltpu.VMEM((FFN_ROWS, D_FF), BF16)],
        compiler_params=_params("arbitrary"),
        name="ffn",
    )(x, g, wg, wu, wd, fin)


ATT_TILE = 512


def _qkv_kernel(x_ref, g_ref, wq_ref, wkt_ref, wv_ref, q_ref, kt_ref, v_ref):
    xn = _rms(x_ref[...], g_ref[...]).astype(BF16)
    q = jnp.dot(xn, wq_ref[...], preferred_element_type=F32) * (HEAD_DIM ** -0.5)
    q = q.astype(BF16)
    kt = lax.dot_general(wkt_ref[...], xn, (((1,), (1,)), ((), ())),
                         preferred_element_type=F32).astype(BF16)
    v = jnp.dot(xn, wv_ref[...], preferred_element_type=F32).astype(BF16)
    ones = jnp.ones((ATT_TILE, V_DIM), BF16)
    for h in range(N_HEADS):
        sl = slice(h * V_DIM, (h + 1) * V_DIM)
        q_ref[0, h] = q[:, sl]
        kt_ref[0, h, 0] = kt[sl, :]
        v_ref[0, h, :, :V_DIM] = v[:, sl]
        v_ref[0, h, :, V_DIM:] = ones


def _qkv(x, g, wq, wkt, wv, batch, seq):
    nt = seq // ATT_TILE
    return pl.pallas_call(
        _qkv_kernel,
        grid=(batch, nt),
        in_specs=[pl.BlockSpec((ATT_TILE, D_MODEL), lambda b, i: (b * nt + i, 0)),
                  _resident((1, D_MODEL)), _resident((D_MODEL, D_MODEL)),
                  _resident((D_MODEL, D_MODEL)), _resident((D_MODEL, D_MODEL))],
        out_specs=[
            pl.BlockSpec((1, N_HEADS, ATT_TILE, V_DIM), lambda b, i: (b, 0, i, 0)),
            pl.BlockSpec((1, N_HEADS, 1, V_DIM, ATT_TILE),
                         lambda b, i: (b, 0, i, 0, 0)),
            pl.BlockSpec((1, N_HEADS, ATT_TILE, 2 * V_DIM),
                         lambda b, i: (b, 0, i, 0)),
        ],
        out_shape=[
            jax.ShapeDtypeStruct((batch, N_HEADS, seq, V_DIM), BF16),
            jax.ShapeDtypeStruct((batch, N_HEADS, nt, V_DIM, ATT_TILE), BF16),
            jax.ShapeDtypeStruct((batch, N_HEADS, seq, 2 * V_DIM), BF16),
        ],
        compiler_params=_params("arbitrary", "arbitrary"),
        name="qkv",
    )(x, g, wq, wkt, wv)


def _attn_kernel(lam_ref, subln_ref, q_ref, kt_ref, v_ref, o_ref, m_ref, acc_ref,
                 *, lambda_init):
    t = ATT_TILE
    qi = pl.program_id(2)
    q = q_ref[0, 0]
    lane = lax.broadcasted_iota(jnp.int32, (t, V_DIM), 1)
    zero = jnp.zeros_like(q)
    qs = jnp.concatenate([jnp.where(lane < HEAD_DIM, q, zero),
                          jnp.where(lane >= HEAD_DIM, q, zero)], axis=0)

    def step(j, masked):
        s = jnp.dot(qs, kt_ref[0, 0, j], preferred_element_type=F32)
        if masked:
            row = lax.broadcasted_iota(jnp.int32, (2 * t, t), 0)
            col = lax.broadcasted_iota(jnp.int32, (2 * t, t), 1)
            row = jnp.where(row >= t, row - t, row)
            s = jnp.where(col <= row, s, -jnp.inf)
        m_prev = m_ref[...]
        m_new = jnp.maximum(m_prev, jnp.max(s, axis=1, keepdims=True))
        alpha = jnp.exp(m_prev - m_new)
        p = jnp.exp(s - jnp.tile(m_new, (1, t // LANES)))
        vj = v_ref[0, 0, pl.ds(pl.multiple_of(j * t, t), t), :]
        pv = jnp.dot(p.astype(BF16), vj, preferred_element_type=F32)
        acc_ref[...] = jnp.tile(alpha, (1, 2)) * acc_ref[...] + pv
        m_ref[...] = m_new

    m_ref[...] = jnp.full_like(m_ref, -jnp.inf)
    acc_ref[...] = jnp.zeros_like(acc_ref)
    step(qi, True)

    def body(j, carry):
        step(j, False)
        return carry

    lax.fori_loop(0, qi, body, 0)

    lam_p = lam_ref[...]
    lam = (jnp.exp(jnp.sum(lam_p[0:1] * lam_p[1:2], axis=1, keepdims=True))
           - jnp.exp(jnp.sum(lam_p[2:3] * lam_p[3:4], axis=1, keepdims=True))
           + lambda_init)
    acc = acc_ref[...]
    o1 = acc[:t, :V_DIM] / acc[:t, V_DIM:]
    o2 = acc[t:, :V_DIM] / acc[t:, V_DIM:]
    o = _rms(o1 - lam * o2, subln_ref[...]) * (1.0 - lambda_init)
    o_ref[0] = o.astype(BF16)


def _attn(lam_p, subln, q, kt, v, batch, seq, lambda_init):
    nt = seq // ATT_TILE
    return pl.pallas_call(
        functools.partial(_attn_kernel, lambda_init=lambda_init),
        grid=(batch, N_HEADS, nt),
        in_specs=[
            _resident((4, HEAD_DIM)), _resident((1, V_DIM)),
            pl.BlockSpec((1, 1, ATT_TILE, V_DIM), lambda b, h, i: (b, h, i, 0)),
            pl.BlockSpec((1, 1, nt, V_DIM, ATT_TILE), lambda b, h, i: (b, h, 0, 0, 0)),
            pl.BlockSpec((1, 1, seq, 2 * V_DIM), lambda b, h, i: (b, h, 0, 0)),
        ],
        out_specs=pl.BlockSpec((1, ATT_TILE, V_DIM), lambda b, h, i: (b, i, h)),
        out_shape=jax.ShapeDtypeStruct((batch, seq, N_HEADS * V_DIM), BF16),
        scratch_shapes=[pltpu.VMEM((2 * ATT_TILE, LANES), F32),
                        pltpu.VMEM((2 * ATT_TILE, 2 * V_DIM), F32)],
        compiler_params=_params("arbitrary", "arbitrary", "arbitrary"),
        name="diff_attn",
    )(lam_p, subln, q, kt, v)


PROJ_ROWS = 512


def _proj_kernel(x_ref, a_ref, w_ref, o_ref):
    o_ref[...] = x_ref[...] + jnp.dot(a_ref[...], w_ref[...],
                                      preferred_element_type=F32)


def _proj_residual(x, a, w):
    rows = x.shape[0]
    row_spec = pl.BlockSpec((PROJ_ROWS, D_MODEL), lambda i: (i, 0))
    return pl.pallas_call(
        _proj_kernel,
        grid=(rows // PROJ_ROWS,),
        in_specs=[row_spec, row_spec, _resident((D_MODEL, D_MODEL))],
        out_specs=row_spec,
        out_shape=jax.ShapeDtypeStruct((rows, D_MODEL), F32),
        compiler_params=_params("arbitrary"),
        name="attn_out_proj",
    )(x, a, w)


CONV_ROWS = 512


def _conv_kernel(x_ref, g_ref, win_ref, wc_ref, wout_ref, o_ref, tail_ref):
    t = CONV_ROWS

    @pl.when(pl.program_id(1) == 0)
    def _():
        tail_ref[...] = jnp.zeros_like(tail_ref)

    x = x_ref[...]
    xn = _rms(x, g_ref[...]).astype(BF16)
    proj = jnp.dot(xn, win_ref[...], preferred_element_type=F32)
    gb = proj[:, :D_MODEL]
    u = proj[:, D_MODEL:2 * D_MODEL] * proj[:, 2 * D_MODEL:]
    tail = tail_ref[...]
    prev1 = tail[SUBLANES - 1:SUBLANES]
    prev2 = tail[SUBLANES - 2:SUBLANES - 1]
    row = lax.broadcasted_iota(jnp.int32, (t, D_MODEL), 0)
    u1 = jnp.where(row == 0, prev1, pltpu.roll(u, 1, 0))
    u2 = jnp.where(row == 0, prev2,
                   jnp.where(row == 1, prev1, pltpu.roll(u, 2, 0)))
    wc = wc_ref[...]
    y = wc[0:1] * u2 + wc[1:2] * u1 + wc[2:3] * u
    tail_ref[...] = u[t - SUBLANES:, :]
    z = (gb * y).astype(BF16)
    o_ref[...] = x + jnp.dot(z, wout_ref[...], preferred_element_type=F32)


def _conv(x, g, w_in, w_conv, w_out, batch, seq):
    nt = seq // CONV_ROWS
    row_spec = pl.BlockSpec((CONV_ROWS, D_MODEL), lambda b, i: (b * nt + i, 0))
    return pl.pallas_call(
        _conv_kernel,
        grid=(batch, nt),
        in_specs=[row_spec, _resident((1, D_MODEL)),
                  _resident((D_MODEL, 3 * D_MODEL)),
                  _resident((CONV_WIDTH, D_MODEL)),
                  _resident((D_MODEL, D_MODEL))],
        out_specs=row_spec,
        out_shape=jax.ShapeDtypeStruct((batch * seq, D_MODEL), F32),
        scratch_shapes=[pltpu.VMEM((SUBLANES, D_MODEL), F32)],
        compiler_params=_params("arbitrary", "arbitrary"),
        name="short_conv",
    )(x, g, w_in, w_conv, w_out)


def kernel(x, ffn1_norm, ffn1_w_gate, ffn1_w_up, ffn1_w_down, mix_norm, attn_w_qkv, attn_lambda_q1, attn_lambda_k1, attn_lambda_q2, attn_lambda_k2, attn_subln, attn_w_out, conv_w_in, conv_w, conv_w_out, ffn2_norm, ffn2_w_gate, ffn2_w_up, ffn2_w_down, final_norm):
    batch, seq, d = x.shape
    depth = ffn1_norm.shape[0]
    bf = lambda w: w.astype(BF16)
    row = lambda v: v.reshape(1, -1)
    fin = row(final_norm)
    h = x.reshape(batch * seq, d)
    for i in range(depth):
        h = _ffn(h, row(ffn1_norm[i]), bf(ffn1_w_gate[i]), bf(ffn1_w_up[i]),
                 bf(ffn1_w_down[i]), fin, False)
        j = i // N_MIXERS
        if i % N_MIXERS == 0:
            w = attn_w_qkv[j]
            q, kt, v = _qkv(h, row(mix_norm[i]), bf(w[:, :d]),
                            bf(w[:, d:2 * d].T), bf(w[:, 2 * d:]), batch, seq)
            lam_p = jnp.stack([attn_lambda_q1[j], attn_lambda_k1[j],
                               attn_lambda_q2[j], attn_lambda_k2[j]])
            lambda_init = 0.8 - 0.6 * math.exp(-0.3 * i)
            a = _attn(lam_p, row(attn_subln[j]), q, kt, v, batch, seq, lambda_init)
            h = _proj_residual(h, a.reshape(batch * seq, d), bf(attn_w_out[j]))
        else:
            h = _conv(h, row(mix_norm[i]), bf(conv_w_in[j]), conv_w[j],
                      bf(conv_w_out[j]), batch, seq)
        h = _ffn(h, row(ffn2_norm[i]), bf(ffn2_w_gate[i]), bf(ffn2_w_up[i]),
                 bf(ffn2_w_down[i]), fin, i == depth - 1)
    return h.reshape(batch, seq, d)
```

```python
import functools
import math

import jax
import jax.numpy as jnp
from jax import lax
from jax.experimental import pallas as pl
from jax.experimental.pallas import tpu as pltpu

D_MODEL = 1024
N_HEADS = 8
HEAD_DIM = 64
V_DIM = 2 * HEAD_DIM
D_FF = 2816
CONV_WIDTH = 3
EPS = 1e-5
N_MIXERS = 2

LANES = 128
SUBLANES = 8
BF16_ROWS = 16
MXU_DIM = 256
VMEM_LIMIT = 56 * 1024 * 1024

F32 = jnp.float32
BF16 = jnp.bfloat16
NT_DIMS = (((1,), (1,)), ((), ()))
LOG2_E = math.log2(math.e)


def _rms(x, g):
    ms = jnp.mean(x * x, axis=-1, keepdims=True)
    return x * lax.rsqrt(ms + EPS) * g


def _resident(shape):
    return pl.BlockSpec(shape, lambda *_: (0,) * len(shape),
                        pipeline_mode=pl.Buffered(1))


def _params(*sem, flags=None):
    return pltpu.CompilerParams(dimension_semantics=sem,
                                vmem_limit_bytes=VMEM_LIMIT, flags=flags)


FFN_ROWS = 512
FFN_CHUNK = MXU_DIM


def _ffn_kernel(*refs, has_proj, has_final):
    refs = list(refs)
    x = refs.pop(0)[...]
    if has_proj:
        a_ref, wo_ref = refs.pop(0), refs.pop(0)
        x = x + jnp.dot(a_ref[...], wo_ref[...], preferred_element_type=F32)
    g_ref, wg_ref, wu_ref, wd_ref = refs[:4]
    o_ref, h_ref = refs[-2:]
    xn = _rms(x, g_ref[...]).astype(BF16)
    for c in range(D_FF // FFN_CHUNK):
        sl = slice(c * FFN_CHUNK, (c + 1) * FFN_CHUNK)
        gate = jnp.dot(xn, wg_ref[:, sl], preferred_element_type=F32)
        up = jnp.dot(xn, wu_ref[:, sl], preferred_element_type=F32)
        h_ref[:, sl] = (gate * jax.nn.sigmoid(gate) * up).astype(BF16)
    y = x + 0.5 * jnp.dot(h_ref[...], wd_ref[...], preferred_element_type=F32)
    if has_final:
        y = _rms(y, refs[4][...])
    o_ref[...] = y


def _ffn(x, ffn_w, proj=None, final_g=None):
    rows = x.shape[0]
    row_spec = pl.BlockSpec((FFN_ROWS, D_MODEL), lambda i: (i, 0))
    args, specs = [x], [row_spec]
    if proj is not None:
        args += list(proj)
        specs += [row_spec, _resident((D_MODEL, D_MODEL))]
    args += list(ffn_w)
    specs += [_resident((1, D_MODEL)), _resident((D_MODEL, D_FF)),
              _resident((D_MODEL, D_FF)), _resident((D_FF, D_MODEL))]
    if final_g is not None:
        args.append(final_g)
        specs.append(_resident((1, D_MODEL)))
    return pl.pallas_call(
        functools.partial(_ffn_kernel, has_proj=proj is not None,
                          has_final=final_g is not None),
        grid=(rows // FFN_ROWS,),
        in_specs=specs,
        out_specs=row_spec,
        out_shape=jax.ShapeDtypeStruct((rows, D_MODEL), F32),
        scratch_shapes=[pltpu.VMEM((FFN_ROWS, D_FF), BF16)],
        compiler_params=_params("arbitrary"),
        name="ffn",
    )(*args)


ATT_TILE = 512
VT_ROWS = V_DIM + BF16_ROWS


def _qkv_kernel(x_ref, g_ref, wqt_ref, wk_ref, wvt_ref, qst_ref, k_ref, vt_ref):
    t = ATT_TILE
    xn = _rms(x_ref[...], g_ref[...]).astype(BF16)
    qt = lax.dot_general(wqt_ref[...], xn, NT_DIMS, preferred_element_type=F32)
    qt = (qt * (HEAD_DIM ** -0.5 * LOG2_E)).astype(BF16)
    k = jnp.dot(xn, wk_ref[...], preferred_element_type=F32).astype(BF16)
    vt = lax.dot_general(wvt_ref[...], xn, NT_DIMS,
                         preferred_element_type=F32).astype(BF16)
    ones = jnp.ones((BF16_ROWS, t), BF16)
    feat = lax.broadcasted_iota(jnp.int32, (V_DIM, t), 0)
    zero = jnp.zeros((V_DIM, t), BF16)
    for h in range(N_HEADS):
        sl = slice(h * V_DIM, (h + 1) * V_DIM)
        qst_ref[0, h, 0, :, :t] = jnp.where(feat < HEAD_DIM, qt[sl, :], zero)
        qst_ref[0, h, 0, :, t:] = jnp.where(feat >= HEAD_DIM, qt[sl, :], zero)
        k_ref[0, h] = k[:, sl]
        vt_ref[0, h, 0, :V_DIM, :] = vt[sl, :]
        vt_ref[0, h, 0, V_DIM:, :] = ones


def _qkv(x, g, wqt, wk, wvt, batch, seq):
    nt = seq // ATT_TILE
    return pl.pallas_call(
        _qkv_kernel,
        grid=(batch, nt),
        in_specs=[pl.BlockSpec((ATT_TILE, D_MODEL), lambda b, i: (b * nt + i, 0)),
                  _resident((1, D_MODEL)), _resident((D_MODEL, D_MODEL)),
                  _resident((D_MODEL, D_MODEL)), _resident((D_MODEL, D_MODEL))],
        out_specs=[
            pl.BlockSpec((1, N_HEADS, 1, V_DIM, 2 * ATT_TILE),
                         lambda b, i: (b, 0, i, 0, 0)),
            pl.BlockSpec((1, N_HEADS, ATT_TILE, V_DIM), lambda b, i: (b, 0, i, 0)),
            pl.BlockSpec((1, N_HEADS, 1, VT_ROWS, ATT_TILE),
                         lambda b, i: (b, 0, i, 0, 0)),
        ],
        out_shape=[
            jax.ShapeDtypeStruct((batch, N_HEADS, nt, V_DIM, 2 * ATT_TILE), BF16),
            jax.ShapeDtypeStruct((batch, N_HEADS, seq, V_DIM), BF16),
            jax.ShapeDtypeStruct((batch, N_HEADS, nt, VT_ROWS, ATT_TILE), BF16),
        ],
        compiler_params=_params("arbitrary", "arbitrary"),
        name="qkv",
    )(x, g, wqt, wk, wvt)


def _causal_pairs(nt):
    return ([(q, q) for q in range(nt)]
            + [(q, k) for q in range(nt) for k in range(q)])


def _attn_kernel(qi_ref, kj_ref, lam_ref, subln_ref, qst_ref, k_ref, vt_ref, o_ref,
                 s0_ref, smax0_ref, p0_ref, alpha0_ref, s1_ref, smax1_ref, p1_ref,
                 alpha1_ref, m_ref, acc_ref, *, lambda_init, nt, n_pairs):
    t = ATT_TILE
    n = n_pairs
    s_ref, smax_ref = (s0_ref, s1_ref), (smax0_ref, smax1_ref)
    p_ref, alpha_ref = (p0_ref, p1_ref), (alpha0_ref, alpha1_ref)

    def stage_a(i, slot, masked):
        kk = k_ref[0, 0, pl.ds(pl.multiple_of(kj_ref[i] * t, t), t), :]
        s = jnp.dot(kk, qst_ref[0, 0, qi_ref[i]],
                    preferred_element_type=F32)
        if masked:
            key = lax.broadcasted_iota(jnp.int32, (t, 2 * t), 0)
            qry = lax.broadcasted_iota(jnp.int32, (t, 2 * t), 1)
            qry = jnp.where(qry >= t, qry - t, qry)
            s = jnp.where(key <= qry, s, -jnp.inf)
        s_ref[slot][...] = s
        smax_ref[slot][...] = jnp.max(s, axis=0, keepdims=True)

    def stage_b(i, slot):
        q = qi_ref[i]
        m_prev = m_ref[q]
        m_new = jnp.maximum(m_prev, smax_ref[slot][...])
        alpha_ref[slot][...] = jnp.exp2(m_prev - m_new)
        p_ref[slot][...] = jnp.exp2(s_ref[slot][...] - m_new).astype(BF16)
        m_ref[q] = m_new

    def stage_c(i, slot):
        q = qi_ref[i]
        pv = jnp.dot(vt_ref[0, 0, kj_ref[i]], p_ref[slot][...],
                     preferred_element_type=F32)
        acc_ref[q] = alpha_ref[slot][...] * acc_ref[q] + pv

    def step(i, slot, masked):
        stage_a(i, slot, masked)
        stage_b(i - 1, 1 - slot)
        stage_c(i - 2, slot)

    def two_steps(masked):
        def body(h, carry):
            step(2 * h, 0, masked)
            step(2 * h + 1, 1, masked)
            return carry
        return body

    def finalize(q, carry):
        lam_p = lam_ref[...]
        lam = (jnp.exp(jnp.sum(lam_p[0:1] * lam_p[1:2], axis=1, keepdims=True))
               - jnp.exp(jnp.sum(lam_p[2:3] * lam_p[3:4], axis=1, keepdims=True))
               + lambda_init)
        acc = acc_ref[q]
        row_sum = acc[V_DIM:V_DIM + 1, :]
        o1 = acc[:V_DIM, :t] / row_sum[:, :t]
        o2 = acc[:V_DIM, t:] / row_sum[:, t:]
        d = o1 - lam * o2
        ms = jnp.mean(d * d, axis=0, keepdims=True)
        o = d * lax.rsqrt(ms + EPS) * (subln_ref[...] * (1.0 - lambda_init))
        o_ref[0, pl.ds(pl.multiple_of(q * t, t), t), :] = o.T.astype(BF16)
        return carry

    assert nt % 2 == 0 and n % 2 == 0 and nt >= 2
    m_ref[...] = jnp.full_like(m_ref, -jnp.inf)
    acc_ref[...] = jnp.zeros_like(acc_ref)
    stage_a(0, 0, True)
    stage_a(1, 1, True)
    stage_b(0, 0)
    lax.fori_loop(1, nt // 2, two_steps(True), 0)
    lax.fori_loop(nt // 2, n // 2, two_steps(False), 0)
    stage_b(n - 1, 1)
    stage_c(n - 2, 0)
    stage_c(n - 1, 1)
    lax.fori_loop(0, nt, finalize, 0)


def _attn(lam_p, subln, qst, k, vt, batch, seq, lambda_init):
    nt = seq // ATT_TILE
    pairs = _causal_pairs(nt)
    qi = jnp.array([p[0] for p in pairs], jnp.int32)
    kj = jnp.array([p[1] for p in pairs], jnp.int32)
    grid_spec = pltpu.PrefetchScalarGridSpec(
        num_scalar_prefetch=2,
        grid=(batch, N_HEADS),
        in_specs=[
            _resident((4, HEAD_DIM)), _resident((V_DIM, 1)),
            pl.BlockSpec((1, 1, nt, V_DIM, 2 * ATT_TILE),
                         lambda b, h, *_: (b, h, 0, 0, 0)),
            pl.BlockSpec((1, 1, seq, V_DIM), lambda b, h, *_: (b, h, 0, 0)),
            pl.BlockSpec((1, 1, nt, VT_ROWS, ATT_TILE),
                         lambda b, h, *_: (b, h, 0, 0, 0)),
        ],
        out_specs=pl.BlockSpec((1, seq, V_DIM), lambda b, h, *_: (b, 0, h)),
        scratch_shapes=2 * [pltpu.VMEM((ATT_TILE, 2 * ATT_TILE), F32),
                            pltpu.VMEM((1, 2 * ATT_TILE), F32),
                            pltpu.VMEM((ATT_TILE, 2 * ATT_TILE), BF16),
                            pltpu.VMEM((1, 2 * ATT_TILE), F32)]
        + [pltpu.VMEM((nt, 1, 2 * ATT_TILE), F32),
           pltpu.VMEM((nt, VT_ROWS, 2 * ATT_TILE), F32)],
    )
    return pl.pallas_call(
        functools.partial(_attn_kernel, lambda_init=lambda_init, nt=nt,
                          n_pairs=len(pairs)),
        grid_spec=grid_spec,
        out_shape=jax.ShapeDtypeStruct((batch, seq, N_HEADS * V_DIM), BF16),
        compiler_params=_params("arbitrary", "arbitrary"),
        name="diff_attn",
    )(qi, kj, lam_p, subln, qst, k, vt)


CONV_ROWS = 512


def _conv_kernel(x_ref, g_ref, win_ref, wc_ref, wout_ref, o_ref, tail_ref):
    t = CONV_ROWS

    @pl.when(pl.program_id(1) == 0)
    def _():
        tail_ref[...] = jnp.zeros_like(tail_ref)

    x = x_ref[...]
    xn = _rms(x, g_ref[...]).astype(BF16)
    proj = jnp.dot(xn, win_ref[...], preferred_element_type=F32)
    gb = proj[:, :D_MODEL]
    u = proj[:, D_MODEL:2 * D_MODEL] * proj[:, 2 * D_MODEL:]
    tail = tail_ref[...]
    prev1 = tail[SUBLANES - 1:SUBLANES]
    prev2 = tail[SUBLANES - 2:SUBLANES - 1]
    row = lax.broadcasted_iota(jnp.int32, (t, D_MODEL), 0)
    u1 = jnp.where(row == 0, prev1, pltpu.roll(u, 1, 0))
    u2 = jnp.where(row == 0, prev2,
                   jnp.where(row == 1, prev1, pltpu.roll(u, 2, 0)))
    wc = wc_ref[...]
    y = wc[0:1] * u2 + wc[1:2] * u1 + wc[2:3] * u
    tail_ref[...] = u[t - SUBLANES:, :]
    z = (gb * y).astype(BF16)
    o_ref[...] = x + jnp.dot(z, wout_ref[...], preferred_element_type=F32)


def _conv(x, g, w_in, w_conv, w_out, batch, seq):
    nt = seq // CONV_ROWS
    row_spec = pl.BlockSpec((CONV_ROWS, D_MODEL), lambda b, i: (b * nt + i, 0))
    return pl.pallas_call(
        _conv_kernel,
        grid=(batch, nt),
        in_specs=[row_spec, _resident((1, D_MODEL)),
                  _resident((D_MODEL, 3 * D_MODEL)),
                  _resident((CONV_WIDTH, D_MODEL)),
                  _resident((D_MODEL, D_MODEL))],
        out_specs=row_spec,
        out_shape=jax.ShapeDtypeStruct((batch * seq, D_MODEL), F32),
        scratch_shapes=[pltpu.VMEM((SUBLANES, D_MODEL), F32)],
        compiler_params=_params("arbitrary", "arbitrary"),
        name="short_conv",
    )(x, g, w_in, w_conv, w_out)


def kernel(x, ffn1_norm, ffn1_w_gate, ffn1_w_up, ffn1_w_down, mix_norm, attn_w_qkv, attn_lambda_q1, attn_lambda_k1, attn_lambda_q2, attn_lambda_k2, attn_subln, attn_w_out, conv_w_in, conv_w, conv_w_out, ffn2_norm, ffn2_w_gate, ffn2_w_up, ffn2_w_down, final_norm):
    batch, seq, d = x.shape
    depth = ffn1_norm.shape[0]
    bf = lambda w: w.astype(BF16)
    row = lambda v: v.reshape(1, -1)
    h = x.reshape(batch * seq, d)
    for i in range(depth):
        h = _ffn(h, (row(ffn1_norm[i]), bf(ffn1_w_gate[i]), bf(ffn1_w_up[i]),
                     bf(ffn1_w_down[i])))
        j = i // N_MIXERS
        proj = None
        if i % N_MIXERS == 0:
            w = attn_w_qkv[j]
            qt, k, vt = _qkv(h, row(mix_norm[i]), bf(w[:, :d].T), bf(w[:, d:2 * d]),
                             bf(w[:, 2 * d:].T), batch, seq)
            lam_p = jnp.stack([attn_lambda_q1[j], attn_lambda_k1[j],
                               attn_lambda_q2[j], attn_lambda_k2[j]])
            lambda_init = 0.8 - 0.6 * math.exp(-0.3 * i)
            a = _attn(lam_p, attn_subln[j].reshape(-1, 1), qt, k, vt, batch, seq,
                      lambda_init)
            proj = (a.reshape(batch * seq, d), bf(attn_w_out[j]))
        else:
            h = _conv(h, row(mix_norm[i]), bf(conv_w_in[j]), conv_w[j],
                      bf(conv_w_out[j]), batch, seq)
        h = _ffn(h, (row(ffn2_norm[i]), bf(ffn2_w_gate[i]), bf(ffn2_w_up[i]),
                     bf(ffn2_w_down[i])), proj=proj,
                 final_g=row(final_norm) if i == depth - 1 else None)
    return h.reshape(batch, seq, d)
```

```python
import functools
import math

import jax
import jax.numpy as jnp
from jax import lax
from jax.experimental import pallas as pl
from jax.experimental.pallas import tpu as pltpu

D_MODEL = 1024
N_HEADS = 8
HEAD_DIM = 64
V_DIM = 2 * HEAD_DIM
D_FF = 2816
CONV_WIDTH = 3
EPS = 1e-5
N_MIXERS = 2

LANES = 128
SUBLANES = 8
BF16_ROWS = 16
MXU_DIM = 256
VMEM_LIMIT = 56 * 1024 * 1024

F32 = jnp.float32
BF16 = jnp.bfloat16
NT_DIMS = (((1,), (1,)), ((), ()))
LOG2_E = math.log2(math.e)


def _rms(x, g):
    ms = jnp.mean(x * x, axis=-1, keepdims=True)
    return x * lax.rsqrt(ms + EPS) * g


def _resident(shape):
    return pl.BlockSpec(shape, lambda *_: (0,) * len(shape),
                        pipeline_mode=pl.Buffered(1))


def _params(*sem, flags=None):
    return pltpu.CompilerParams(dimension_semantics=sem,
                                vmem_limit_bytes=VMEM_LIMIT, flags=flags)


FFN_ROWS = 512
FFN_CHUNK = MXU_DIM


def _ffn_kernel(*refs, has_proj, has_final):
    refs = list(refs)
    x = refs.pop(0)[...]
    if has_proj:
        a_ref, wo_ref = refs.pop(0), refs.pop(0)
        x = x + jnp.dot(a_ref[...], wo_ref[...], preferred_element_type=F32)
    g_ref, wg_ref, wu_ref, wd_ref = refs[:4]
    o_ref, h_ref = refs[-2:]
    xn = _rms(x, g_ref[...]).astype(BF16)
    for c in range(D_FF // FFN_CHUNK):
        sl = slice(c * FFN_CHUNK, (c + 1) * FFN_CHUNK)
        gate = jnp.dot(xn, wg_ref[:, sl], preferred_element_type=F32)
        up = jnp.dot(xn, wu_ref[:, sl], preferred_element_type=F32)
        h_ref[:, sl] = (gate * jax.nn.sigmoid(gate) * up).astype(BF16)
    y = x + 0.5 * jnp.dot(h_ref[...], wd_ref[...], preferred_element_type=F32)
    if has_final:
        y = _rms(y, refs[4][...])
    o_ref[...] = y


def _ffn(x, ffn_w, proj=None, final_g=None):
    rows = x.shape[0]
    row_spec = pl.BlockSpec((FFN_ROWS, D_MODEL), lambda i: (i, 0))
    args, specs = [x], [row_spec]
    if proj is not None:
        args += list(proj)
        specs += [row_spec, _resident((D_MODEL, D_MODEL))]
    args += list(ffn_w)
    specs += [_resident((1, D_MODEL)), _resident((D_MODEL, D_FF)),
              _resident((D_MODEL, D_FF)), _resident((D_FF, D_MODEL))]
    if final_g is not None:
        args.append(final_g)
        specs.append(_resident((1, D_MODEL)))
    return pl.pallas_call(
        functools.partial(_ffn_kernel, has_proj=proj is not None,
                          has_final=final_g is not None),
        grid=(rows // FFN_ROWS,),
        in_specs=specs,
        out_specs=row_spec,
        out_shape=jax.ShapeDtypeStruct((rows, D_MODEL), F32),
        scratch_shapes=[pltpu.VMEM((FFN_ROWS, D_FF), BF16)],
        compiler_params=_params("arbitrary"),
        name="ffn",
    )(*args)


ATT_TILE = 512
VT_ROWS = V_DIM + BF16_ROWS
STAT_QNORM, STAT_DIAG, STAT_KNORM, STAT_ROWS_USED = 0, 1, 2, 3
BOUND_SLACK = 1.001
SAFE_GAP = 64.0


def _qkv_kernel(x_ref, g_ref, wqt_ref, wk_ref, wvt_ref, qst_ref, k_ref, vt_ref,
                stat_ref):
    t = ATT_TILE
    xn = _rms(x_ref[...], g_ref[...]).astype(BF16)
    qt = lax.dot_general(wqt_ref[...], xn, NT_DIMS, preferred_element_type=F32)
    qt = (qt * (HEAD_DIM ** -0.5 * LOG2_E)).astype(BF16)
    k = jnp.dot(xn, wk_ref[...], preferred_element_type=F32).astype(BF16)
    vt = lax.dot_general(wvt_ref[...], xn, NT_DIMS,
                         preferred_element_type=F32).astype(BF16)
    ones = jnp.ones((BF16_ROWS, t), BF16)
    feat = lax.broadcasted_iota(jnp.int32, (V_DIM, t), 0)
    zero = jnp.zeros((V_DIM, t), BF16)
    qf = qt.astype(F32)
    kf = k.astype(F32).T
    col_norm = lambda a: jnp.sqrt(jnp.sum(a * a, axis=0, keepdims=True))
    for h in range(N_HEADS):
        sl = slice(h * V_DIM, (h + 1) * V_DIM)
        for c in range(2):
            rows = slice(h * V_DIM + c * HEAD_DIM, h * V_DIM + (c + 1) * HEAD_DIM)
            cols = slice(c * t, (c + 1) * t)
            q_c, k_c = qf[rows, :], kf[rows, :]
            stat_ref[0, h, 0, STAT_QNORM:STAT_QNORM + 1, cols] = col_norm(q_c)
            stat_ref[0, h, 0, STAT_DIAG:STAT_DIAG + 1, cols] = jnp.sum(
                q_c * k_c, axis=0, keepdims=True)
            stat_ref[0, h, 0, STAT_KNORM:STAT_KNORM + 1, cols] = col_norm(k_c)
        stat_ref[0, h, 0, STAT_ROWS_USED:, :] = jnp.zeros(
            (SUBLANES - STAT_ROWS_USED, 2 * t), F32)
        qst_ref[0, h, 0, :, :t] = jnp.where(feat < HEAD_DIM, qt[sl, :], zero)
        qst_ref[0, h, 0, :, t:] = jnp.where(feat >= HEAD_DIM, qt[sl, :], zero)
        k_ref[0, h] = k[:, sl]
        vt_ref[0, h, 0, :V_DIM, :] = vt[sl, :]
        vt_ref[0, h, 0, V_DIM:, :] = ones


def _qkv(x, g, wqt, wk, wvt, batch, seq):
    nt = seq // ATT_TILE
    return pl.pallas_call(
        _qkv_kernel,
        grid=(batch, nt),
        in_specs=[pl.BlockSpec((ATT_TILE, D_MODEL), lambda b, i: (b * nt + i, 0)),
                  _resident((1, D_MODEL)), _resident((D_MODEL, D_MODEL)),
                  _resident((D_MODEL, D_MODEL)), _resident((D_MODEL, D_MODEL))],
        out_specs=[
            pl.BlockSpec((1, N_HEADS, 1, V_DIM, 2 * ATT_TILE),
                         lambda b, i: (b, 0, i, 0, 0)),
            pl.BlockSpec((1, N_HEADS, ATT_TILE, V_DIM), lambda b, i: (b, 0, i, 0)),
            pl.BlockSpec((1, N_HEADS, 1, VT_ROWS, ATT_TILE),
                         lambda b, i: (b, 0, i, 0, 0)),
            pl.BlockSpec((1, N_HEADS, 1, SUBLANES, 2 * ATT_TILE),
                         lambda b, i: (b, 0, i, 0, 0)),
        ],
        out_shape=[
            jax.ShapeDtypeStruct((batch, N_HEADS, nt, V_DIM, 2 * ATT_TILE), BF16),
            jax.ShapeDtypeStruct((batch, N_HEADS, seq, V_DIM), BF16),
            jax.ShapeDtypeStruct((batch, N_HEADS, nt, VT_ROWS, ATT_TILE), BF16),
            jax.ShapeDtypeStruct((batch, N_HEADS, nt, SUBLANES, 2 * ATT_TILE), F32),
        ],
        compiler_params=_params("arbitrary", "arbitrary"),
        name="qkv",
    )(x, g, wqt, wk, wvt)


def _causal_pairs(nt):
    return ([(q, q) for q in range(nt)]
            + [(q, k) for q in range(nt) for k in range(q)])


def _attn_kernel(qi_ref, kj_ref, lam_ref, subln_ref, qst_ref, k_ref, vt_ref, stat_ref,
                 o_ref, s0_ref, smax0_ref, p0_ref, alpha0_ref, s1_ref, smax1_ref,
                 p1_ref, alpha1_ref, m_ref, acc_ref, *, lambda_init, nt, n_pairs):
    t = ATT_TILE
    n = n_pairs
    s_ref, smax_ref = (s0_ref, s1_ref), (smax0_ref, smax1_ref)
    p_ref, alpha_ref = (p0_ref, p1_ref), (alpha0_ref, alpha1_ref)

    def scores(i, masked):
        kk = k_ref[0, 0, pl.ds(pl.multiple_of(kj_ref[i] * t, t), t), :]
        s = jnp.dot(kk, qst_ref[0, 0, qi_ref[i]],
                    preferred_element_type=F32)
        if masked:
            key = lax.broadcasted_iota(jnp.int32, (t, 2 * t), 0)
            qry = lax.broadcasted_iota(jnp.int32, (t, 2 * t), 1)
            qry = jnp.where(qry >= t, qry - t, qry)
            s = jnp.where(key <= qry, s, -jnp.inf)
        return s

    def pv(i, slot):
        return jnp.dot(vt_ref[0, 0, kj_ref[i]], p_ref[slot][...],
                       preferred_element_type=F32)

    def fast_a(i, slot, masked):
        p_ref[slot][...] = jnp.exp2(scores(i, masked) - m_ref[qi_ref[i]]).astype(BF16)

    def fast_c(i, slot):
        q = qi_ref[i]
        acc_ref[q] = acc_ref[q] + pv(i, slot)

    def fast_step(i, slot, masked):
        fast_a(i, slot, masked)
        fast_c(i - 1, 1 - slot)

    def slow_a(i, slot, masked):
        s = scores(i, masked)
        s_ref[slot][...] = s
        smax_ref[slot][...] = jnp.max(s, axis=0, keepdims=True)

    def slow_b(i, slot):
        q = qi_ref[i]
        m_prev = m_ref[q]
        m_new = jnp.maximum(m_prev, smax_ref[slot][...])
        alpha_ref[slot][...] = jnp.exp2(m_prev - m_new)
        p_ref[slot][...] = jnp.exp2(s_ref[slot][...] - m_new).astype(BF16)
        m_ref[q] = m_new

    def slow_c(i, slot):
        q = qi_ref[i]
        acc_ref[q] = alpha_ref[slot][...] * acc_ref[q] + pv(i, slot)

    def slow_step(i, slot, masked):
        slow_a(i, slot, masked)
        slow_b(i - 1, 1 - slot)
        slow_c(i - 2, slot)

    def two_steps(step, masked):
        def body(h, carry):
            step(2 * h, 0, masked)
            step(2 * h + 1, 1, masked)
            return carry
        return body

    assert nt % 2 == 0 and n % 2 == 0 and nt >= 2

    def fast_path():
        fast_a(0, 0, True)
        fast_step(1, 1, True)
        lax.fori_loop(1, nt // 2, two_steps(fast_step, True), 0)
        lax.fori_loop(nt // 2, n // 2, two_steps(fast_step, False), 0)
        fast_c(n - 1, 1)

    def slow_path():
        m_ref[...] = jnp.full_like(m_ref, -jnp.inf)
        slow_a(0, 0, True)
        slow_a(1, 1, True)
        slow_b(0, 0)
        lax.fori_loop(1, nt // 2, two_steps(slow_step, True), 0)
        lax.fori_loop(nt // 2, n // 2, two_steps(slow_step, False), 0)
        slow_b(n - 1, 1)
        slow_c(n - 2, 0)
        slow_c(n - 1, 1)

    stat = stat_ref[0, 0]
    k_norm = jnp.max(stat[:, STAT_KNORM:STAT_KNORM + 1, :], axis=0)
    lane = lax.broadcasted_iota(jnp.int32, (1, 2 * t), 1)
    neg = jnp.full_like(k_norm, -jnp.inf)
    k_max = jnp.where(
        lane < t,
        jnp.max(jnp.where(lane < t, k_norm, neg), axis=1, keepdims=True),
        jnp.max(jnp.where(lane >= t, k_norm, neg), axis=1, keepdims=True))
    shift = stat[:, STAT_QNORM:STAT_QNORM + 1, :] * (k_max * BOUND_SLACK)
    gap = shift - stat[:, STAT_DIAG:STAT_DIAG + 1, :]
    safe = jnp.max(jnp.where(gap <= SAFE_GAP, 0.0, 1.0)) == 0.0
    m_ref[...] = shift
    acc_ref[...] = jnp.zeros_like(acc_ref)
    lax.cond(safe, fast_path, slow_path)

    def finalize(q, carry):
        lam_p = lam_ref[...]
        lam = (jnp.exp(jnp.sum(lam_p[0:1] * lam_p[1:2], axis=1, keepdims=True))
               - jnp.exp(jnp.sum(lam_p[2:3] * lam_p[3:4], axis=1, keepdims=True))
               + lambda_init)
        acc = acc_ref[q]
        row_sum = acc[V_DIM:V_DIM + 1, :]
        o1 = acc[:V_DIM, :t] / row_sum[:, :t]
        o2 = acc[:V_DIM, t:] / row_sum[:, t:]
        d = o1 - lam * o2
        ms = jnp.mean(d * d, axis=0, keepdims=True)
        o = d * lax.rsqrt(ms + EPS) * (subln_ref[...] * (1.0 - lambda_init))
        o_ref[0, pl.ds(pl.multiple_of(q * t, t), t), :] = o.T.astype(BF16)
        return carry

    lax.fori_loop(0, nt, finalize, 0)


def _attn(lam_p, subln, qst, k, vt, stat, batch, seq, lambda_init):
    nt = seq // ATT_TILE
    pairs = _causal_pairs(nt)
    qi = jnp.array([p[0] for p in pairs], jnp.int32)
    kj = jnp.array([p[1] for p in pairs], jnp.int32)
    grid_spec = pltpu.PrefetchScalarGridSpec(
        num_scalar_prefetch=2,
        grid=(batch, N_HEADS),
        in_specs=[
            _resident((4, HEAD_DIM)), _resident((V_DIM, 1)),
            pl.BlockSpec((1, 1, nt, V_DIM, 2 * ATT_TILE),
                         lambda b, h, *_: (b, h, 0, 0, 0)),
            pl.BlockSpec((1, 1, seq, V_DIM), lambda b, h, *_: (b, h, 0, 0)),
            pl.BlockSpec((1, 1, nt, VT_ROWS, ATT_TILE),
                         lambda b, h, *_: (b, h, 0, 0, 0)),
            pl.BlockSpec((1, 1, nt, SUBLANES, 2 * ATT_TILE),
                         lambda b, h, *_: (b, h, 0, 0, 0)),
        ],
        out_specs=pl.BlockSpec((1, seq, V_DIM), lambda b, h, *_: (b, 0, h)),
        scratch_shapes=2 * [pltpu.VMEM((ATT_TILE, 2 * ATT_TILE), F32),
                            pltpu.VMEM((1, 2 * ATT_TILE), F32),
                            pltpu.VMEM((ATT_TILE, 2 * ATT_TILE), BF16),
                            pltpu.VMEM((1, 2 * ATT_TILE), F32)]
        + [pltpu.VMEM((nt, 1, 2 * ATT_TILE), F32),
           pltpu.VMEM((nt, VT_ROWS, 2 * ATT_TILE), F32)],
    )
    return pl.pallas_call(
        functools.partial(_attn_kernel, lambda_init=lambda_init, nt=nt,
                          n_pairs=len(pairs)),
        grid_spec=grid_spec,
        out_shape=jax.ShapeDtypeStruct((batch, seq, N_HEADS * V_DIM), BF16),
        compiler_params=_params("arbitrary", "arbitrary"),
        name="diff_attn",
    )(qi, kj, lam_p, subln, qst, k, vt, stat)


CONV_ROWS = 512


def _conv_kernel(x_ref, g_ref, win_ref, wc_ref, wout_ref, o_ref, tail_ref):
    t = CONV_ROWS

    @pl.when(pl.program_id(1) == 0)
    def _():
        tail_ref[...] = jnp.zeros_like(tail_ref)

    x = x_ref[...]
    xn = _rms(x, g_ref[...]).astype(BF16)
    proj = jnp.dot(xn, win_ref[...], preferred_element_type=F32)
    gb = proj[:, :D_MODEL]
    u = proj[:, D_MODEL:2 * D_MODEL] * proj[:, 2 * D_MODEL:]
    tail = tail_ref[...]
    prev1 = tail[SUBLANES - 1:SUBLANES]
    prev2 = tail[SUBLANES - 2:SUBLANES - 1]
    row = lax.broadcasted_iota(jnp.int32, (t, D_MODEL), 0)
    u1 = jnp.where(row == 0, prev1, pltpu.roll(u, 1, 0))
    u2 = jnp.where(row == 0, prev2,
                   jnp.where(row == 1, prev1, pltpu.roll(u, 2, 0)))
    wc = wc_ref[...]
    y = wc[0:1] * u2 + wc[1:2] * u1 + wc[2:3] * u
    tail_ref[...] = u[t - SUBLANES:, :]
    z = (gb * y).astype(BF16)
    o_ref[...] = x + jnp.dot(z, wout_ref[...], preferred_element_type=F32)


def _conv(x, g, w_in, w_conv, w_out, batch, seq):
    nt = seq // CONV_ROWS
    row_spec = pl.BlockSpec((CONV_ROWS, D_MODEL), lambda b, i: (b * nt + i, 0))
    return pl.pallas_call(
        _conv_kernel,
        grid=(batch, nt),
        in_specs=[row_spec, _resident((1, D_MODEL)),
                  _resident((D_MODEL, 3 * D_MODEL)),
                  _resident((CONV_WIDTH, D_MODEL)),
                  _resident((D_MODEL, D_MODEL))],
        out_specs=row_spec,
        out_shape=jax.ShapeDtypeStruct((batch * seq, D_MODEL), F32),
        scratch_shapes=[pltpu.VMEM((SUBLANES, D_MODEL), F32)],
        compiler_params=_params("arbitrary", "arbitrary"),
        name="short_conv",
    )(x, g, w_in, w_conv, w_out)


def kernel(x, ffn1_norm, ffn1_w_gate, ffn1_w_up, ffn1_w_down, mix_norm, attn_w_qkv, attn_lambda_q1, attn_lambda_k1, attn_lambda_q2, attn_lambda_k2, attn_subln, attn_w_out, conv_w_in, conv_w, conv_w_out, ffn2_norm, ffn2_w_gate, ffn2_w_up, ffn2_w_down, final_norm):
    batch, seq, d = x.shape
    depth = ffn1_norm.shape[0]
    bf = lambda w: w.astype(BF16)
    row = lambda v: v.reshape(1, -1)
    h = x.reshape(batch * seq, d)
    for i in range(depth):
        h = _ffn(h, (row(ffn1_norm[i]), bf(ffn1_w_gate[i]), bf(ffn1_w_up[i]),
                     bf(ffn1_w_down[i])))
        j = i // N_MIXERS
        proj = None
        if i % N_MIXERS == 0:
            w = attn_w_qkv[j]
            qst, k, vt, stat = _qkv(h, row(mix_norm[i]), bf(w[:, :d].T),
                                    bf(w[:, d:2 * d]), bf(w[:, 2 * d:].T), batch, seq)
            lam_p = jnp.stack([attn_lambda_q1[j], attn_lambda_k1[j],
                               attn_lambda_q2[j], attn_lambda_k2[j]])
            lambda_init = 0.8 - 0.6 * math.exp(-0.3 * i)
            a = _attn(lam_p, attn_subln[j].reshape(-1, 1), qst, k, vt, stat, batch,
                      seq, lambda_init)
            proj = (a.reshape(batch * seq, d), bf(attn_w_out[j]))
        else:
            h = _conv(h, row(mix_norm[i]), bf(conv_w_in[j]), conv_w[j],
                      bf(conv_w_out[j]), batch, seq)
        h = _ffn(h, (row(ffn2_norm[i]), bf(ffn2_w_gate[i]), bf(ffn2_w_up[i]),
                     bf(ffn2_w_down[i])), proj=proj,
                 final_g=row(final_norm) if i == depth - 1 else None)
    return h.reshape(batch, seq, d)
```

```python
import functools
import math

import jax
import jax.numpy as jnp
from jax import lax
from jax.experimental import pallas as pl
from jax.experimental.pallas import tpu as pltpu

D_MODEL = 1024
N_HEADS = 8
HEAD_DIM = 64
V_DIM = 2 * HEAD_DIM
D_FF = 2816
CONV_WIDTH = 3
EPS = 1e-5
N_MIXERS = 2

LANES = 128
SUBLANES = 8
BF16_ROWS = 16
MXU_DIM = 256
VMEM_LIMIT = 56 * 1024 * 1024

F32 = jnp.float32
BF16 = jnp.bfloat16
NT_DIMS = (((1,), (1,)), ((), ()))
LOG2_E = math.log2(math.e)


def _rms(x, g):
    ms = jnp.mean(x * x, axis=-1, keepdims=True)
    return x * lax.rsqrt(ms + EPS) * g


def _resident(shape):
    return pl.BlockSpec(shape, lambda *_: (0,) * len(shape),
                        pipeline_mode=pl.Buffered(1))


def _params(*sem, flags=None):
    return pltpu.CompilerParams(dimension_semantics=sem,
                                vmem_limit_bytes=VMEM_LIMIT, flags=flags)


FFN_ROWS = 512
FFN_CHUNK = MXU_DIM


def _ffn_kernel(*refs, has_proj, has_final):
    refs = list(refs)
    x = refs.pop(0)[...]
    if has_proj:
        a_ref, wo_ref = refs.pop(0), refs.pop(0)
        x = x + jnp.dot(a_ref[...], wo_ref[...], preferred_element_type=F32)
    g_ref, wg_ref, wu_ref, wd_ref = refs[:4]
    o_ref, h_ref = refs[-2:]
    xn = _rms(x, g_ref[...]).astype(BF16)
    for c in range(D_FF // FFN_CHUNK):
        sl = slice(c * FFN_CHUNK, (c + 1) * FFN_CHUNK)
        gate = jnp.dot(xn, wg_ref[:, sl], preferred_element_type=F32)
        up = jnp.dot(xn, wu_ref[:, sl], preferred_element_type=F32)
        h_ref[:, sl] = (gate * jax.nn.sigmoid(gate) * up).astype(BF16)
    y = x + 0.5 * jnp.dot(h_ref[...], wd_ref[...], preferred_element_type=F32)
    if has_final:
        y = _rms(y, refs[4][...])
    o_ref[...] = y


def _ffn(x, ffn_w, proj=None, final_g=None):
    rows = x.shape[0]
    row_spec = pl.BlockSpec((FFN_ROWS, D_MODEL), lambda i: (i, 0))
    args, specs = [x], [row_spec]
    if proj is not None:
        args += list(proj)
        specs += [row_spec, _resident((D_MODEL, D_MODEL))]
    args += list(ffn_w)
    specs += [_resident((1, D_MODEL)), _resident((D_MODEL, D_FF)),
              _resident((D_MODEL, D_FF)), _resident((D_FF, D_MODEL))]
    if final_g is not None:
        args.append(final_g)
        specs.append(_resident((1, D_MODEL)))
    return pl.pallas_call(
        functools.partial(_ffn_kernel, has_proj=proj is not None,
                          has_final=final_g is not None),
        grid=(rows // FFN_ROWS,),
        in_specs=specs,
        out_specs=row_spec,
        out_shape=jax.ShapeDtypeStruct((rows, D_MODEL), F32),
        scratch_shapes=[pltpu.VMEM((FFN_ROWS, D_FF), BF16)],
        compiler_params=_params("arbitrary"),
        name="ffn",
    )(*args)


ATT_TILE = 512
STAT_QNORM, STAT_DIAG, STAT_KNORM, STAT_ROWS_USED = 0, 1, 2, 3
BOUND_SLACK = 1.001
SAFE_GAP = 64.0
FAST_UNROLL = 8


def _qkv_kernel(x_ref, g_ref, wqt_ref, wk_ref, wvt_ref, qst_ref, k_ref, vt_ref,
                stat_ref):
    t = ATT_TILE
    xn = _rms(x_ref[...], g_ref[...]).astype(BF16)
    qt = lax.dot_general(wqt_ref[...], xn, NT_DIMS, preferred_element_type=F32)
    qt = (qt * (HEAD_DIM ** -0.5 * LOG2_E)).astype(BF16)
    k = jnp.dot(xn, wk_ref[...], preferred_element_type=F32).astype(BF16)
    vt = lax.dot_general(wvt_ref[...], xn, NT_DIMS,
                         preferred_element_type=F32).astype(BF16)
    feat = lax.broadcasted_iota(jnp.int32, (V_DIM, t), 0)
    zero = jnp.zeros((V_DIM, t), BF16)
    qf = qt.astype(F32)
    kf = k.astype(F32).T
    col_norm = lambda a: jnp.sqrt(jnp.sum(a * a, axis=0, keepdims=True))
    for h in range(N_HEADS):
        sl = slice(h * V_DIM, (h + 1) * V_DIM)
        for c in range(2):
            rows = slice(h * V_DIM + c * HEAD_DIM, h * V_DIM + (c + 1) * HEAD_DIM)
            cols = slice(c * t, (c + 1) * t)
            q_c, k_c = qf[rows, :], kf[rows, :]
            stat_ref[0, h, 0, STAT_QNORM:STAT_QNORM + 1, cols] = col_norm(q_c)
            stat_ref[0, h, 0, STAT_DIAG:STAT_DIAG + 1, cols] = jnp.sum(
                q_c * k_c, axis=0, keepdims=True)
            stat_ref[0, h, 0, STAT_KNORM:STAT_KNORM + 1, cols] = col_norm(k_c)
        stat_ref[0, h, 0, STAT_ROWS_USED:, :] = jnp.zeros(
            (SUBLANES - STAT_ROWS_USED, 2 * t), F32)
        qst_ref[0, h, 0, :, :t] = jnp.where(feat < HEAD_DIM, qt[sl, :], zero)
        qst_ref[0, h, 0, :, t:] = jnp.where(feat >= HEAD_DIM, qt[sl, :], zero)
        k_ref[0, h] = k[:, sl]
        vt_ref[0, h, 0] = vt[sl, :]


def _qkv(x, g, wqt, wk, wvt, batch, seq):
    nt = seq // ATT_TILE
    return pl.pallas_call(
        _qkv_kernel,
        grid=(batch, nt),
        in_specs=[pl.BlockSpec((ATT_TILE, D_MODEL), lambda b, i: (b * nt + i, 0)),
                  _resident((1, D_MODEL)), _resident((D_MODEL, D_MODEL)),
                  _resident((D_MODEL, D_MODEL)), _resident((D_MODEL, D_MODEL))],
        out_specs=[
            pl.BlockSpec((1, N_HEADS, 1, V_DIM, 2 * ATT_TILE),
                         lambda b, i: (b, 0, i, 0, 0)),
            pl.BlockSpec((1, N_HEADS, ATT_TILE, V_DIM), lambda b, i: (b, 0, i, 0)),
            pl.BlockSpec((1, N_HEADS, 1, V_DIM, ATT_TILE),
                         lambda b, i: (b, 0, i, 0, 0)),
            pl.BlockSpec((1, N_HEADS, 1, SUBLANES, 2 * ATT_TILE),
                         lambda b, i: (b, 0, i, 0, 0)),
        ],
        out_shape=[
            jax.ShapeDtypeStruct((batch, N_HEADS, nt, V_DIM, 2 * ATT_TILE), BF16),
            jax.ShapeDtypeStruct((batch, N_HEADS, seq, V_DIM), BF16),
            jax.ShapeDtypeStruct((batch, N_HEADS, nt, V_DIM, ATT_TILE), BF16),
            jax.ShapeDtypeStruct((batch, N_HEADS, nt, SUBLANES, 2 * ATT_TILE), F32),
        ],
        compiler_params=_params("arbitrary", "arbitrary"),
        name="qkv",
    )(x, g, wqt, wk, wvt)


def _causal_pairs(nt):
    return ([(q, q) for q in range(nt)]
            + [(q, k) for q in range(nt) for k in range(q)])


def _attn_kernel(qi_ref, kj_ref, lam_ref, subln_ref, qst_ref, k_ref, vt_ref, stat_ref,
                 o_ref, s0_ref, smax0_ref, p0_ref, alpha0_ref, s1_ref, smax1_ref,
                 p1_ref, alpha1_ref, m_ref, l_ref, acc_ref, *, lambda_init, nt,
                 n_pairs):
    t = ATT_TILE
    n = n_pairs
    s_ref, smax_ref = (s0_ref, s1_ref), (smax0_ref, smax1_ref)
    p_ref, alpha_ref = (p0_ref, p1_ref), (alpha0_ref, alpha1_ref)

    def scores(i, masked):
        kk = k_ref[0, 0, pl.ds(pl.multiple_of(kj_ref[i] * t, t), t), :]
        s = jnp.dot(kk, qst_ref[0, 0, qi_ref[i]],
                    preferred_element_type=F32)
        if masked:
            key = lax.broadcasted_iota(jnp.int32, (t, 2 * t), 0)
            qry = lax.broadcasted_iota(jnp.int32, (t, 2 * t), 1)
            qry = jnp.where(qry >= t, qry - t, qry)
            s = jnp.where(key <= qry, s, -jnp.inf)
        return s

    def key_sum(p):
        return jnp.sum(p.reshape(t // SUBLANES, SUBLANES, 2 * t), axis=0)

    def pv(i, slot):
        return jnp.dot(vt_ref[0, 0, kj_ref[i]], p_ref[slot][...],
                       preferred_element_type=F32)

    def fast_a(i, slot, masked):
        q = qi_ref[i]
        p = jnp.exp2(scores(i, masked) - m_ref[q])
        l_ref[q] = l_ref[q] + key_sum(p)
        p_ref[slot][...] = p.astype(BF16)

    def fast_c(i, slot):
        q = qi_ref[i]
        acc_ref[q] = acc_ref[q] + pv(i, slot)

    def fast_step(i, slot, masked):
        fast_a(i, slot, masked)
        fast_c(i - 1, 1 - slot)

    def slow_a(i, slot, masked):
        s = scores(i, masked)
        s_ref[slot][...] = s
        smax_ref[slot][...] = jnp.max(s, axis=0, keepdims=True)

    def slow_b(i, slot):
        q = qi_ref[i]
        m_prev = m_ref[q]
        m_new = jnp.maximum(m_prev, smax_ref[slot][...])
        alpha = jnp.exp2(m_prev - m_new)
        p = jnp.exp2(s_ref[slot][...] - m_new)
        l_ref[q] = alpha * l_ref[q] + key_sum(p)
        alpha_ref[slot][...] = alpha
        p_ref[slot][...] = p.astype(BF16)
        m_ref[q] = m_new

    def slow_c(i, slot):
        q = qi_ref[i]
        acc_ref[q] = alpha_ref[slot][...] * acc_ref[q] + pv(i, slot)

    def slow_step(i, slot, masked):
        slow_a(i, slot, masked)
        slow_b(i - 1, 1 - slot)
        slow_c(i - 2, slot)

    def multi_steps(step, masked, pairs_per_body):
        def body(h, carry):
            for u in range(pairs_per_body):
                step(pairs_per_body * h + u, u % 2, masked)
            return carry
        return body

    def two_steps(step, masked):
        return multi_steps(step, masked, 2)

    assert nt % 2 == 0 and n % 2 == 0 and nt >= 2

    def fast_path():
        fast_a(0, 0, True)
        fast_step(1, 1, True)
        lax.fori_loop(1, nt // 2, two_steps(fast_step, True), 0)
        lax.fori_loop(nt // FAST_UNROLL, n // FAST_UNROLL,
                      multi_steps(fast_step, False, FAST_UNROLL), 0)
        fast_c(n - 1, 1)

    def slow_path():
        m_ref[...] = jnp.full_like(m_ref, -jnp.inf)
        slow_a(0, 0, True)
        slow_a(1, 1, True)
        slow_b(0, 0)
        lax.fori_loop(1, nt // 2, two_steps(slow_step, True), 0)
        lax.fori_loop(nt // 2, n // 2, two_steps(slow_step, False), 0)
        slow_b(n - 1, 1)
        slow_c(n - 2, 0)
        slow_c(n - 1, 1)

    stat = stat_ref[0, 0]
    k_norm = jnp.max(stat[:, STAT_KNORM:STAT_KNORM + 1, :], axis=0)
    lane = lax.broadcasted_iota(jnp.int32, (1, 2 * t), 1)
    neg = jnp.full_like(k_norm, -jnp.inf)
    k_max = jnp.where(
        lane < t,
        jnp.max(jnp.where(lane < t, k_norm, neg), axis=1, keepdims=True),
        jnp.max(jnp.where(lane >= t, k_norm, neg), axis=1, keepdims=True))
    shift = stat[:, STAT_QNORM:STAT_QNORM + 1, :] * (k_max * BOUND_SLACK)
    gap = shift - stat[:, STAT_DIAG:STAT_DIAG + 1, :]
    safe = jnp.max(jnp.where(gap <= SAFE_GAP, 0.0, 1.0)) == 0.0
    m_ref[...] = shift
    l_ref[...] = jnp.zeros_like(l_ref)
    acc_ref[...] = jnp.zeros_like(acc_ref)
    lax.cond(safe, fast_path, slow_path)

    def finalize(q, carry):
        lam_p = lam_ref[...]
        lam = (jnp.exp(jnp.sum(lam_p[0:1] * lam_p[1:2], axis=1, keepdims=True))
               - jnp.exp(jnp.sum(lam_p[2:3] * lam_p[3:4], axis=1, keepdims=True))
               + lambda_init)
        acc = acc_ref[q]
        row_sum = jnp.sum(l_ref[q], axis=0, keepdims=True)
        o1 = acc[:, :t] / row_sum[:, :t]
        o2 = acc[:, t:] / row_sum[:, t:]
        d = o1 - lam * o2
        ms = jnp.mean(d * d, axis=0, keepdims=True)
        o = d * lax.rsqrt(ms + EPS) * (subln_ref[...] * (1.0 - lambda_init))
        o_ref[0, pl.ds(pl.multiple_of(q * t, t), t), :] = o.T.astype(BF16)
        return carry

    lax.fori_loop(0, nt, finalize, 0)


def _attn(lam_p, subln, qst, k, vt, stat, batch, seq, lambda_init):
    nt = seq // ATT_TILE
    pairs = _causal_pairs(nt)
    qi = jnp.array([p[0] for p in pairs], jnp.int32)
    kj = jnp.array([p[1] for p in pairs], jnp.int32)
    grid_spec = pltpu.PrefetchScalarGridSpec(
        num_scalar_prefetch=2,
        grid=(batch, N_HEADS),
        in_specs=[
            _resident((4, HEAD_DIM)), _resident((V_DIM, 1)),
            pl.BlockSpec((1, 1, nt, V_DIM, 2 * ATT_TILE),
                         lambda b, h, *_: (b, h, 0, 0, 0)),
            pl.BlockSpec((1, 1, seq, V_DIM), lambda b, h, *_: (b, h, 0, 0)),
            pl.BlockSpec((1, 1, nt, V_DIM, ATT_TILE),
                         lambda b, h, *_: (b, h, 0, 0, 0)),
            pl.BlockSpec((1, 1, nt, SUBLANES, 2 * ATT_TILE),
                         lambda b, h, *_: (b, h, 0, 0, 0)),
        ],
        out_specs=pl.BlockSpec((1, seq, V_DIM), lambda b, h, *_: (b, 0, h)),
        scratch_shapes=2 * [pltpu.VMEM((ATT_TILE, 2 * ATT_TILE), F32),
                            pltpu.VMEM((1, 2 * ATT_TILE), F32),
                            pltpu.VMEM((ATT_TILE, 2 * ATT_TILE), BF16),
                            pltpu.VMEM((1, 2 * ATT_TILE), F32)]
        + [pltpu.VMEM((nt, 1, 2 * ATT_TILE), F32),
           pltpu.VMEM((nt, SUBLANES, 2 * ATT_TILE), F32),
           pltpu.VMEM((nt, V_DIM, 2 * ATT_TILE), F32)],
    )
    return pl.pallas_call(
        functools.partial(_attn_kernel, lambda_init=lambda_init, nt=nt,
                          n_pairs=len(pairs)),
        grid_spec=grid_spec,
        out_shape=jax.ShapeDtypeStruct((batch, seq, N_HEADS * V_DIM), BF16),
        compiler_params=_params("arbitrary", "arbitrary"),
        name="diff_attn",
    )(qi, kj, lam_p, subln, qst, k, vt, stat)


CONV_ROWS = 512


def _conv_kernel(x_ref, g_ref, win_ref, wc_ref, wout_ref, o_ref, tail_ref):
    t = CONV_ROWS

    @pl.when(pl.program_id(1) == 0)
    def _():
        tail_ref[...] = jnp.zeros_like(tail_ref)

    x = x_ref[...]
    xn = _rms(x, g_ref[...]).astype(BF16)
    proj = jnp.dot(xn, win_ref[...], preferred_element_type=F32)
    gb = proj[:, :D_MODEL]
    u = proj[:, D_MODEL:2 * D_MODEL] * proj[:, 2 * D_MODEL:]
    tail = tail_ref[...]
    prev1 = tail[SUBLANES - 1:SUBLANES]
    prev2 = tail[SUBLANES - 2:SUBLANES - 1]
    row = lax.broadcasted_iota(jnp.int32, (t, D_MODEL), 0)
    u1 = jnp.where(row == 0, prev1, pltpu.roll(u, 1, 0))
    u2 = jnp.where(row == 0, prev2,
                   jnp.where(row == 1, prev1, pltpu.roll(u, 2, 0)))
    wc = wc_ref[...]
    y = wc[0:1] * u2 + wc[1:2] * u1 + wc[2:3] * u
    tail_ref[...] = u[t - SUBLANES:, :]
    z = (gb * y).astype(BF16)
    o_ref[...] = x + jnp.dot(z, wout_ref[...], preferred_element_type=F32)


def _conv(x, g, w_in, w_conv, w_out, batch, seq):
    nt = seq // CONV_ROWS
    row_spec = pl.BlockSpec((CONV_ROWS, D_MODEL), lambda b, i: (b * nt + i, 0))
    return pl.pallas_call(
        _conv_kernel,
        grid=(batch, nt),
        in_specs=[row_spec, _resident((1, D_MODEL)),
                  _resident((D_MODEL, 3 * D_MODEL)),
                  _resident((CONV_WIDTH, D_MODEL)),
                  _resident((D_MODEL, D_MODEL))],
        out_specs=row_spec,
        out_shape=jax.ShapeDtypeStruct((batch * seq, D_MODEL), F32),
        scratch_shapes=[pltpu.VMEM((SUBLANES, D_MODEL), F32)],
        compiler_params=_params("arbitrary", "arbitrary"),
        name="short_conv",
    )(x, g, w_in, w_conv, w_out)


def kernel(x, ffn1_norm, ffn1_w_gate, ffn1_w_up, ffn1_w_down, mix_norm, attn_w_qkv, attn_lambda_q1, attn_lambda_k1, attn_lambda_q2, attn_lambda_k2, attn_subln, attn_w_out, conv_w_in, conv_w, conv_w_out, ffn2_norm, ffn2_w_gate, ffn2_w_up, ffn2_w_down, final_norm):
    batch, seq, d = x.shape
    depth = ffn1_norm.shape[0]
    bf = lambda w: w.astype(BF16)
    row = lambda v: v.reshape(1, -1)
    h = x.reshape(batch * seq, d)
    for i in range(depth):
        h = _ffn(h, (row(ffn1_norm[i]), bf(ffn1_w_gate[i]), bf(ffn1_w_up[i]),
                     bf(ffn1_w_down[i])))
        j = i // N_MIXERS
        proj = None
        if i % N_MIXERS == 0:
            w = attn_w_qkv[j]
            qst, k, vt, stat = _qkv(h, row(mix_norm[i]), bf(w[:, :d].T),
                                    bf(w[:, d:2 * d]), bf(w[:, 2 * d:].T), batch, seq)
            lam_p = jnp.stack([attn_lambda_q1[j], attn_lambda_k1[j],
                               attn_lambda_q2[j], attn_lambda_k2[j]])
            lambda_init = 0.8 - 0.6 * math.exp(-0.3 * i)
            a = _attn(lam_p, attn_subln[j].reshape(-1, 1), qst, k, vt, stat, batch,
                      seq, lambda_init)
            proj = (a.reshape(batch * seq, d), bf(attn_w_out[j]))
        else:
            h = _conv(h, row(mix_norm[i]), bf(conv_w_in[j]), conv_w[j],
                      bf(conv_w_out[j]), batch, seq)
        h = _ffn(h, (row(ffn2_norm[i]), bf(ffn2_w_gate[i]), bf(ffn2_w_up[i]),
                     bf(ffn2_w_down[i])), proj=proj,
                 final_g=row(final_norm) if i == depth - 1 else None)
    return h.reshape(batch, seq, d)
```

```python
import functools
import math

import jax
import jax.numpy as jnp
from jax import lax
from jax.experimental import pallas as pl
from jax.experimental.pallas import tpu as pltpu

D_MODEL = 1024
N_HEADS = 8
HEAD_DIM = 64
V_DIM = 2 * HEAD_DIM
D_FF = 2816
CONV_WIDTH = 3
EPS = 1e-5
N_MIXERS = 2

LANES = 128
SUBLANES = 8
BF16_ROWS = 16
MXU_DIM = 256
VMEM_LIMIT = 56 * 1024 * 1024

F32 = jnp.float32
BF16 = jnp.bfloat16
NT_DIMS = (((1,), (1,)), ((), ()))
LOG2_E = math.log2(math.e)


def _rms(x, g):
    ms = jnp.mean(x * x, axis=-1, keepdims=True)
    return x * lax.rsqrt(ms + EPS) * g


def _resident(shape):
    return pl.BlockSpec(shape, lambda *_: (0,) * len(shape),
                        pipeline_mode=pl.Buffered(1))


def _params(*sem, flags=None):
    return pltpu.CompilerParams(dimension_semantics=sem,
                                vmem_limit_bytes=VMEM_LIMIT, flags=flags)


FFN_ROWS = 1024
FFN_CHUNK = MXU_DIM


def _ffn_kernel(*refs, has_proj, has_final):
    refs = list(refs)
    x = refs.pop(0)[...]
    if has_proj:
        a_ref, wo_ref = refs.pop(0), refs.pop(0)
        x = x + jnp.dot(a_ref[...], wo_ref[...], preferred_element_type=F32)
    g_ref, wg_ref, wu_ref, wd_ref = refs[:4]
    o_ref, h_ref = refs[-2:]
    xn = _rms(x, g_ref[...]).astype(BF16)
    for c in range(D_FF // FFN_CHUNK):
        sl = slice(c * FFN_CHUNK, (c + 1) * FFN_CHUNK)
        gate = jnp.dot(xn, wg_ref[:, sl], preferred_element_type=F32)
        up = jnp.dot(xn, wu_ref[:, sl], preferred_element_type=F32)
        h_ref[:, sl] = (gate * jax.nn.sigmoid(gate) * up).astype(BF16)
    y = x + 0.5 * jnp.dot(h_ref[...], wd_ref[...], preferred_element_type=F32)
    if has_final:
        y = _rms(y, refs[4][...])
    o_ref[...] = y


def _ffn(x, ffn_w, proj=None, final_g=None):
    rows = x.shape[0]
    row_spec = pl.BlockSpec((FFN_ROWS, D_MODEL), lambda i: (i, 0))
    args, specs = [x], [row_spec]
    if proj is not None:
        args += list(proj)
        specs += [row_spec, _resident((D_MODEL, D_MODEL))]
    args += list(ffn_w)
    specs += [_resident((1, D_MODEL)), _resident((D_MODEL, D_FF)),
              _resident((D_MODEL, D_FF)), _resident((D_FF, D_MODEL))]
    if final_g is not None:
        args.append(final_g)
        specs.append(_resident((1, D_MODEL)))
    return pl.pallas_call(
        functools.partial(_ffn_kernel, has_proj=proj is not None,
                          has_final=final_g is not None),
        grid=(rows // FFN_ROWS,),
        in_specs=specs,
        out_specs=row_spec,
        out_shape=jax.ShapeDtypeStruct((rows, D_MODEL), F32),
        scratch_shapes=[pltpu.VMEM((FFN_ROWS, D_FF), BF16)],
        compiler_params=_params("arbitrary"),
        name="ffn",
    )(*args)


ATT_TILE = 512
STAT_QNORM, STAT_DIAG, STAT_KNORM, STAT_ROWS_USED = 0, 1, 2, 3
BOUND_SLACK = 1.001
SAFE_GAP = 64.0
FAST_UNROLL = 8


def _qkv_kernel(x_ref, g_ref, wqt_ref, wk_ref, wvt_ref, qst_ref, k_ref, vt_ref,
                stat_ref):
    t = ATT_TILE
    xn = _rms(x_ref[...], g_ref[...]).astype(BF16)
    qt = lax.dot_general(wqt_ref[...], xn, NT_DIMS, preferred_element_type=F32)
    qt = (qt * (HEAD_DIM ** -0.5 * LOG2_E)).astype(BF16)
    k = jnp.dot(xn, wk_ref[...], preferred_element_type=F32).astype(BF16)
    vt = lax.dot_general(wvt_ref[...], xn, NT_DIMS,
                         preferred_element_type=F32).astype(BF16)
    feat = lax.broadcasted_iota(jnp.int32, (V_DIM, t), 0)
    zero = jnp.zeros((V_DIM, t), BF16)
    qf = qt.astype(F32)
    kf = k.astype(F32).T
    col_norm = lambda a: jnp.sqrt(jnp.sum(a * a, axis=0, keepdims=True))
    for h in range(N_HEADS):
        sl = slice(h * V_DIM, (h + 1) * V_DIM)
        for c in range(2):
            rows = slice(h * V_DIM + c * HEAD_DIM, h * V_DIM + (c + 1) * HEAD_DIM)
            cols = slice(c * t, (c + 1) * t)
            q_c, k_c = qf[rows, :], kf[rows, :]
            stat_ref[0, h, 0, STAT_QNORM:STAT_QNORM + 1, cols] = col_norm(q_c)
            stat_ref[0, h, 0, STAT_DIAG:STAT_DIAG + 1, cols] = jnp.sum(
                q_c * k_c, axis=0, keepdims=True)
            stat_ref[0, h, 0, STAT_KNORM:STAT_KNORM + 1, cols] = col_norm(k_c)
        stat_ref[0, h, 0, STAT_ROWS_USED:, :] = jnp.zeros(
            (SUBLANES - STAT_ROWS_USED, 2 * t), F32)
        qst_ref[0, h, 0, :, :t] = jnp.where(feat < HEAD_DIM, qt[sl, :], zero)
        qst_ref[0, h, 0, :, t:] = jnp.where(feat >= HEAD_DIM, qt[sl, :], zero)
        k_ref[0, h] = k[:, sl]
        vt_ref[0, h, 0] = vt[sl, :]


def _qkv(x, g, wqt, wk, wvt, batch, seq):
    nt = seq // ATT_TILE
    return pl.pallas_call(
        _qkv_kernel,
        grid=(batch, nt),
        in_specs=[pl.BlockSpec((ATT_TILE, D_MODEL), lambda b, i: (b * nt + i, 0)),
                  _resident((1, D_MODEL)), _resident((D_MODEL, D_MODEL)),
                  _resident((D_MODEL, D_MODEL)), _resident((D_MODEL, D_MODEL))],
        out_specs=[
            pl.BlockSpec((1, N_HEADS, 1, V_DIM, 2 * ATT_TILE),
                         lambda b, i: (b, 0, i, 0, 0)),
            pl.BlockSpec((1, N_HEADS, ATT_TILE, V_DIM), lambda b, i: (b, 0, i, 0)),
            pl.BlockSpec((1, N_HEADS, 1, V_DIM, ATT_TILE),
                         lambda b, i: (b, 0, i, 0, 0)),
            pl.BlockSpec((1, N_HEADS, 1, SUBLANES, 2 * ATT_TILE),
                         lambda b, i: (b, 0, i, 0, 0)),
        ],
        out_shape=[
            jax.ShapeDtypeStruct((batch, N_HEADS, nt, V_DIM, 2 * ATT_TILE), BF16),
            jax.ShapeDtypeStruct((batch, N_HEADS, seq, V_DIM), BF16),
            jax.ShapeDtypeStruct((batch, N_HEADS, nt, V_DIM, ATT_TILE), BF16),
            jax.ShapeDtypeStruct((batch, N_HEADS, nt, SUBLANES, 2 * ATT_TILE), F32),
        ],
        compiler_params=_params("arbitrary", "arbitrary"),
        name="qkv",
    )(x, g, wqt, wk, wvt)


def _causal_pairs(nt):
    return ([(q, q) for q in range(nt)]
            + [(q, k) for q in range(nt) for k in range(q)])


def _attn_kernel(qi_ref, kj_ref, lam_ref, subln_ref, qst_ref, k_ref, vt_ref, stat_ref,
                 o_ref, s0_ref, smax0_ref, p0_ref, alpha0_ref, s1_ref, smax1_ref,
                 p1_ref, alpha1_ref, m_ref, l_ref, acc_ref, *, lambda_init, nt,
                 n_pairs):
    t = ATT_TILE
    n = n_pairs
    s_ref, smax_ref = (s0_ref, s1_ref), (smax0_ref, smax1_ref)
    p_ref, alpha_ref = (p0_ref, p1_ref), (alpha0_ref, alpha1_ref)

    def scores(i, masked):
        kk = k_ref[0, 0, pl.ds(pl.multiple_of(kj_ref[i] * t, t), t), :]
        s = jnp.dot(kk, qst_ref[0, 0, qi_ref[i]],
                    preferred_element_type=F32)
        if masked:
            key = lax.broadcasted_iota(jnp.int32, (t, 2 * t), 0)
            qry = lax.broadcasted_iota(jnp.int32, (t, 2 * t), 1)
            qry = jnp.where(qry >= t, qry - t, qry)
            s = jnp.where(key <= qry, s, -jnp.inf)
        return s

    def key_sum(p):
        return jnp.sum(p.reshape(t // SUBLANES, SUBLANES, p.shape[1]), axis=0)

    def pv(i, slot):
        return jnp.dot(vt_ref[0, 0, kj_ref[i]], p_ref[slot][...],
                       preferred_element_type=F32)

    halves = (slice(0, t), slice(t, 2 * t))

    def fast_a(i, slot, masked, cols):
        q = qi_ref[i]
        kk = k_ref[0, 0, pl.ds(pl.multiple_of(kj_ref[i] * t, t), t), :]
        s = jnp.dot(kk, qst_ref[0, 0, q, :, cols], preferred_element_type=F32)
        if masked:
            key = lax.broadcasted_iota(jnp.int32, (t, t), 0)
            qry = lax.broadcasted_iota(jnp.int32, (t, t), 1)
            s = jnp.where(key <= qry, s, -jnp.inf)
        p = jnp.exp2(s - m_ref[q, :, cols])
        l_ref[q, :, cols] = l_ref[q, :, cols] + key_sum(p)
        p_ref[slot][:, cols] = p.astype(BF16)

    def fast_c(i, slot, cols):
        q = qi_ref[i]
        acc_ref[q, :, cols] = acc_ref[q, :, cols] + jnp.dot(
            vt_ref[0, 0, kj_ref[i]], p_ref[slot][:, cols], preferred_element_type=F32)

    def fast_step(i, slot, masked):
        for cols in halves:
            fast_a(i, slot, masked, cols)
            fast_c(i - 1, 1 - slot, cols)

    def slow_a(i, slot, masked):
        s = scores(i, masked)
        s_ref[slot][...] = s
        smax_ref[slot][...] = jnp.max(s, axis=0, keepdims=True)

    def slow_b(i, slot):
        q = qi_ref[i]
        m_prev = m_ref[q]
        m_new = jnp.maximum(m_prev, smax_ref[slot][...])
        alpha = jnp.exp2(m_prev - m_new)
        p = jnp.exp2(s_ref[slot][...] - m_new)
        l_ref[q] = alpha * l_ref[q] + key_sum(p)
        alpha_ref[slot][...] = alpha
        p_ref[slot][...] = p.astype(BF16)
        m_ref[q] = m_new

    def slow_c(i, slot):
        q = qi_ref[i]
        acc_ref[q] = alpha_ref[slot][...] * acc_ref[q] + pv(i, slot)

    def slow_step(i, slot, masked):
        slow_a(i, slot, masked)
        slow_b(i - 1, 1 - slot)
        slow_c(i - 2, slot)

    def multi_steps(step, masked, pairs_per_body):
        def body(h, carry):
            for u in range(pairs_per_body):
                step(pairs_per_body * h + u, u % 2, masked)
            return carry
        return body

    def two_steps(step, masked):
        return multi_steps(step, masked, 2)

    assert nt % 2 == 0 and n % 2 == 0 and nt >= 2

    def fast_path():
        for cols in halves:
            fast_a(0, 0, True, cols)
        fast_step(1, 1, True)
        lax.fori_loop(1, nt // 2, two_steps(fast_step, True), 0)
        lax.fori_loop(nt // FAST_UNROLL, n // FAST_UNROLL,
                      multi_steps(fast_step, False, FAST_UNROLL), 0)
        for cols in halves:
            fast_c(n - 1, 1, cols)

    def slow_path():
        m_ref[...] = jnp.full_like(m_ref, -jnp.inf)
        slow_a(0, 0, True)
        slow_a(1, 1, True)
        slow_b(0, 0)
        lax.fori_loop(1, nt // 2, two_steps(slow_step, True), 0)
        lax.fori_loop(nt // 2, n // 2, two_steps(slow_step, False), 0)
        slow_b(n - 1, 1)
        slow_c(n - 2, 0)
        slow_c(n - 1, 1)

    stat = stat_ref[0, 0]
    k_norm = jnp.max(stat[:, STAT_KNORM:STAT_KNORM + 1, :], axis=0)
    lane = lax.broadcasted_iota(jnp.int32, (1, 2 * t), 1)
    neg = jnp.full_like(k_norm, -jnp.inf)
    k_max = jnp.where(
        lane < t,
        jnp.max(jnp.where(lane < t, k_norm, neg), axis=1, keepdims=True),
        jnp.max(jnp.where(lane >= t, k_norm, neg), axis=1, keepdims=True))
    shift = stat[:, STAT_QNORM:STAT_QNORM + 1, :] * (k_max * BOUND_SLACK)
    gap = shift - stat[:, STAT_DIAG:STAT_DIAG + 1, :]
    safe = jnp.max(jnp.where(gap <= SAFE_GAP, 0.0, 1.0)) == 0.0
    m_ref[...] = shift
    l_ref[...] = jnp.zeros_like(l_ref)
    acc_ref[...] = jnp.zeros_like(acc_ref)
    lax.cond(safe, fast_path, slow_path)

    def finalize(q, carry):
        lam_p = lam_ref[...]
        lam = (jnp.exp(jnp.sum(lam_p[0:1] * lam_p[1:2], axis=1, keepdims=True))
               - jnp.exp(jnp.sum(lam_p[2:3] * lam_p[3:4], axis=1, keepdims=True))
               + lambda_init)
        acc = acc_ref[q]
        row_sum = jnp.sum(l_ref[q], axis=0, keepdims=True)
        o1 = acc[:, :t] / row_sum[:, :t]
        o2 = acc[:, t:] / row_sum[:, t:]
        d = o1 - lam * o2
        ms = jnp.mean(d * d, axis=0, keepdims=True)
        o = d * lax.rsqrt(ms + EPS) * (subln_ref[...] * (1.0 - lambda_init))
        o_ref[0, pl.ds(pl.multiple_of(q * t, t), t), :] = o.T.astype(BF16)
        return carry

    lax.fori_loop(0, nt, finalize, 0)


def _attn(lam_p, subln, qst, k, vt, stat, batch, seq, lambda_init):
    nt = seq // ATT_TILE
    pairs = _causal_pairs(nt)
    qi = jnp.array([p[0] for p in pairs], jnp.int32)
    kj = jnp.array([p[1] for p in pairs], jnp.int32)
    grid_spec = pltpu.PrefetchScalarGridSpec(
        num_scalar_prefetch=2,
        grid=(batch, N_HEADS),
        in_specs=[
            _resident((4, HEAD_DIM)), _resident((V_DIM, 1)),
            pl.BlockSpec((1, 1, nt, V_DIM, 2 * ATT_TILE),
                         lambda b, h, *_: (b, h, 0, 0, 0)),
            pl.BlockSpec((1, 1, seq, V_DIM), lambda b, h, *_: (b, h, 0, 0)),
            pl.BlockSpec((1, 1, nt, V_DIM, ATT_TILE),
                         lambda b, h, *_: (b, h, 0, 0, 0)),
            pl.BlockSpec((1, 1, nt, SUBLANES, 2 * ATT_TILE),
                         lambda b, h, *_: (b, h, 0, 0, 0)),
        ],
        out_specs=pl.BlockSpec((1, seq, V_DIM), lambda b, h, *_: (b, 0, h)),
        scratch_shapes=2 * [pltpu.VMEM((ATT_TILE, 2 * ATT_TILE), F32),
                            pltpu.VMEM((1, 2 * ATT_TILE), F32),
                            pltpu.VMEM((ATT_TILE, 2 * ATT_TILE), BF16),
                            pltpu.VMEM((1, 2 * ATT_TILE), F32)]
        + [pltpu.VMEM((nt, 1, 2 * ATT_TILE), F32),
           pltpu.VMEM((nt, SUBLANES, 2 * ATT_TILE), F32),
           pltpu.VMEM((nt, V_DIM, 2 * ATT_TILE), F32)],
    )
    return pl.pallas_call(
        functools.partial(_attn_kernel, lambda_init=lambda_init, nt=nt,
                          n_pairs=len(pairs)),
        grid_spec=grid_spec,
        out_shape=jax.ShapeDtypeStruct((batch, seq, N_HEADS * V_DIM), BF16),
        compiler_params=_params("arbitrary", "arbitrary"),
        name="diff_attn",
    )(qi, kj, lam_p, subln, qst, k, vt, stat)


CONV_ROWS = 512


def _conv_kernel(x_ref, g_ref, win_ref, wc_ref, wout_ref, o_ref, tail_ref):
    t = CONV_ROWS

    @pl.when(pl.program_id(1) == 0)
    def _():
        tail_ref[...] = jnp.zeros_like(tail_ref)

    x = x_ref[...]
    xn = _rms(x, g_ref[...]).astype(BF16)
    proj = jnp.dot(xn, win_ref[...], preferred_element_type=F32)
    gb = proj[:, :D_MODEL]
    u = proj[:, D_MODEL:2 * D_MODEL] * proj[:, 2 * D_MODEL:]
    tail = tail_ref[...]
    prev1 = tail[SUBLANES - 1:SUBLANES]
    prev2 = tail[SUBLANES - 2:SUBLANES - 1]
    row = lax.broadcasted_iota(jnp.int32, (t, D_MODEL), 0)
    u1 = jnp.where(row == 0, prev1, pltpu.roll(u, 1, 0))
    u2 = jnp.where(row == 0, prev2,
                   jnp.where(row == 1, prev1, pltpu.roll(u, 2, 0)))
    wc = wc_ref[...]
    y = wc[0:1] * u2 + wc[1:2] * u1 + wc[2:3] * u
    tail_ref[...] = u[t - SUBLANES:, :]
    z = (gb * y).astype(BF16)
    o_ref[...] = x + jnp.dot(z, wout_ref[...], preferred_element_type=F32)


def _conv(x, g, w_in, w_conv, w_out, batch, seq):
    nt = seq // CONV_ROWS
    row_spec = pl.BlockSpec((CONV_ROWS, D_MODEL), lambda b, i: (b * nt + i, 0))
    return pl.pallas_call(
        _conv_kernel,
        grid=(batch, nt),
        in_specs=[row_spec, _resident((1, D_MODEL)),
                  _resident((D_MODEL, 3 * D_MODEL)),
                  _resident((CONV_WIDTH, D_MODEL)),
                  _resident((D_MODEL, D_MODEL))],
        out_specs=row_spec,
        out_shape=jax.ShapeDtypeStruct((batch * seq, D_MODEL), F32),
        scratch_shapes=[pltpu.VMEM((SUBLANES, D_MODEL), F32)],
        compiler_params=_params("arbitrary", "arbitrary"),
        name="short_conv",
    )(x, g, w_in, w_conv, w_out)


def kernel(x, ffn1_norm, ffn1_w_gate, ffn1_w_up, ffn1_w_down, mix_norm, attn_w_qkv, attn_lambda_q1, attn_lambda_k1, attn_lambda_q2, attn_lambda_k2, attn_subln, attn_w_out, conv_w_in, conv_w, conv_w_out, ffn2_norm, ffn2_w_gate, ffn2_w_up, ffn2_w_down, final_norm):
    batch, seq, d = x.shape
    depth = ffn1_norm.shape[0]
    bf = lambda w: w.astype(BF16)
    row = lambda v: v.reshape(1, -1)
    h = x.reshape(batch * seq, d)
    for i in range(depth):
        h = _ffn(h, (row(ffn1_norm[i]), bf(ffn1_w_gate[i]), bf(ffn1_w_up[i]),
                     bf(ffn1_w_down[i])))
        j = i // N_MIXERS
        proj = None
        if i % N_MIXERS == 0:
            w = attn_w_qkv[j]
            qst, k, vt, stat = _qkv(h, row(mix_norm[i]), bf(w[:, :d].T),
                                    bf(w[:, d:2 * d]), bf(w[:, 2 * d:].T), batch, seq)
            lam_p = jnp.stack([attn_lambda_q1[j], attn_lambda_k1[j],
                               attn_lambda_q2[j], attn_lambda_k2[j]])
            lambda_init = 0.8 - 0.6 * math.exp(-0.3 * i)
            a = _attn(lam_p, attn_subln[j].reshape(-1, 1), qst, k, vt, stat, batch,
                      seq, lambda_init)
            proj = (a.reshape(batch * seq, d), bf(attn_w_out[j]))
        else:
            h = _conv(h, row(mix_norm[i]), bf(conv_w_in[j]), conv_w[j],
                      bf(conv_w_out[j]), batch, seq)
        h = _ffn(h, (row(ffn2_norm[i]), bf(ffn2_w_gate[i]), bf(ffn2_w_up[i]),
                     bf(ffn2_w_down[i])), proj=proj,
                 final_g=row(final_norm) if i == depth - 1 else None)
    return h.reshape(batch, seq, d)
```

```python
import functools
import math

import jax
import jax.numpy as jnp
from jax import lax
from jax.experimental import pallas as pl
from jax.experimental.pallas import tpu as pltpu

D_MODEL = 1024
N_HEADS = 8
HEAD_DIM = 64
V_DIM = 2 * HEAD_DIM
D_FF = 2816
CONV_WIDTH = 3
EPS = 1e-5
N_MIXERS = 2

LANES = 128
SUBLANES = 8
BF16_ROWS = 16
MXU_DIM = 256
VMEM_LIMIT = 56 * 1024 * 1024

F32 = jnp.float32
BF16 = jnp.bfloat16
NT_DIMS = (((1,), (1,)), ((), ()))
LOG2_E = math.log2(math.e)


def _rms(x, g):
    ms = jnp.mean(x * x, axis=-1, keepdims=True)
    return x * lax.rsqrt(ms + EPS) * g


def _resident(shape):
    return pl.BlockSpec(shape, lambda *_: (0,) * len(shape),
                        pipeline_mode=pl.Buffered(1))


def _params(*sem, flags=None):
    return pltpu.CompilerParams(dimension_semantics=sem,
                                vmem_limit_bytes=VMEM_LIMIT, flags=flags)


FFN_ROWS = 1024
FFN_CHUNK = MXU_DIM


def _ffn_kernel(*refs, has_proj, has_final):
    refs = list(refs)
    x = refs.pop(0)[...]
    if has_proj:
        a_ref, wo_ref = refs.pop(0), refs.pop(0)
        x = x + jnp.dot(a_ref[...], wo_ref[...], preferred_element_type=F32)
    g_ref, wg_ref, wu_ref, wd_ref = refs[:4]
    o_ref, h_ref = refs[-2:]
    xn = _rms(x, g_ref[...]).astype(BF16)
    for c in range(D_FF // FFN_CHUNK):
        sl = slice(c * FFN_CHUNK, (c + 1) * FFN_CHUNK)
        gate = jnp.dot(xn, wg_ref[:, sl], preferred_element_type=F32)
        up = jnp.dot(xn, wu_ref[:, sl], preferred_element_type=F32)
        h_ref[:, sl] = (gate * jax.nn.sigmoid(gate) * up).astype(BF16)
    y = x + 0.5 * jnp.dot(h_ref[...], wd_ref[...], preferred_element_type=F32)
    if has_final:
        y = _rms(y, refs[4][...])
    o_ref[...] = y


def _ffn(x, ffn_w, proj=None, final_g=None):
    rows = x.shape[0]
    row_spec = pl.BlockSpec((FFN_ROWS, D_MODEL), lambda i: (i, 0))
    args, specs = [x], [row_spec]
    if proj is not None:
        args += list(proj)
        specs += [row_spec, _resident((D_MODEL, D_MODEL))]
    args += list(ffn_w)
    specs += [_resident((1, D_MODEL)), _resident((D_MODEL, D_FF)),
              _resident((D_MODEL, D_FF)), _resident((D_FF, D_MODEL))]
    if final_g is not None:
        args.append(final_g)
        specs.append(_resident((1, D_MODEL)))
    return pl.pallas_call(
        functools.partial(_ffn_kernel, has_proj=proj is not None,
                          has_final=final_g is not None),
        grid=(rows // FFN_ROWS,),
        in_specs=specs,
        out_specs=row_spec,
        out_shape=jax.ShapeDtypeStruct((rows, D_MODEL), F32),
        scratch_shapes=[pltpu.VMEM((FFN_ROWS, D_FF), BF16)],
        compiler_params=_params("arbitrary"),
        name="ffn",
    )(*args)


ATT_TILE = 512
STAT_QNORM, STAT_DIAG, STAT_KNORM, STAT_ROWS_USED = 0, 1, 2, 3
BOUND_SLACK = 1.001
SAFE_GAP = 64.0
FAST_UNROLL = 8


def _qkv_kernel(x_ref, g_ref, wqt_ref, wk_ref, wvt_ref, qst_ref, k_ref, vt_ref,
                stat_ref):
    t = ATT_TILE
    xn = _rms(x_ref[...], g_ref[...]).astype(BF16)
    qt = lax.dot_general(wqt_ref[...], xn, NT_DIMS, preferred_element_type=F32)
    qt = (qt * (HEAD_DIM ** -0.5 * LOG2_E)).astype(BF16)
    k = jnp.dot(xn, wk_ref[...], preferred_element_type=F32).astype(BF16)
    vt = lax.dot_general(wvt_ref[...], xn, NT_DIMS,
                         preferred_element_type=F32).astype(BF16)
    feat = lax.broadcasted_iota(jnp.int32, (V_DIM, t), 0)
    zero = jnp.zeros((V_DIM, t), BF16)
    qf = qt.astype(F32)
    kf = k.astype(F32).T
    col_norm = lambda a: jnp.sqrt(jnp.sum(a * a, axis=0, keepdims=True))
    for h in range(N_HEADS):
        sl = slice(h * V_DIM, (h + 1) * V_DIM)
        for c in range(2):
            rows = slice(h * V_DIM + c * HEAD_DIM, h * V_DIM + (c + 1) * HEAD_DIM)
            cols = slice(c * t, (c + 1) * t)
            q_c, k_c = qf[rows, :], kf[rows, :]
            stat_ref[0, h, 0, STAT_QNORM:STAT_QNORM + 1, cols] = col_norm(q_c)
            stat_ref[0, h, 0, STAT_DIAG:STAT_DIAG + 1, cols] = jnp.sum(
                q_c * k_c, axis=0, keepdims=True)
            stat_ref[0, h, 0, STAT_KNORM:STAT_KNORM + 1, cols] = col_norm(k_c)
        stat_ref[0, h, 0, STAT_ROWS_USED:, :] = jnp.zeros(
            (SUBLANES - STAT_ROWS_USED, 2 * t), F32)
        qst_ref[0, h, 0, :, :t] = jnp.where(feat < HEAD_DIM, qt[sl, :], zero)
        qst_ref[0, h, 0, :, t:] = jnp.where(feat >= HEAD_DIM, qt[sl, :], zero)
        k_ref[0, h] = k[:, sl]
        vt_ref[0, h, 0] = vt[sl, :]


def _qkv(x, g, wqt, wk, wvt, batch, seq):
    nt = seq // ATT_TILE
    return pl.pallas_call(
        _qkv_kernel,
        grid=(batch, nt),
        in_specs=[pl.BlockSpec((ATT_TILE, D_MODEL), lambda b, i: (b * nt + i, 0)),
                  _resident((1, D_MODEL)), _resident((D_MODEL, D_MODEL)),
                  _resident((D_MODEL, D_MODEL)), _resident((D_MODEL, D_MODEL))],
        out_specs=[
            pl.BlockSpec((1, N_HEADS, 1, V_DIM, 2 * ATT_TILE),
                         lambda b, i: (b, 0, i, 0, 0)),
            pl.BlockSpec((1, N_HEADS, ATT_TILE, V_DIM), lambda b, i: (b, 0, i, 0)),
            pl.BlockSpec((1, N_HEADS, 1, V_DIM, ATT_TILE),
                         lambda b, i: (b, 0, i, 0, 0)),
            pl.BlockSpec((1, N_HEADS, 1, SUBLANES, 2 * ATT_TILE),
                         lambda b, i: (b, 0, i, 0, 0)),
        ],
        out_shape=[
            jax.ShapeDtypeStruct((batch, N_HEADS, nt, V_DIM, 2 * ATT_TILE), BF16),
            jax.ShapeDtypeStruct((batch, N_HEADS, seq, V_DIM), BF16),
            jax.ShapeDtypeStruct((batch, N_HEADS, nt, V_DIM, ATT_TILE), BF16),
            jax.ShapeDtypeStruct((batch, N_HEADS, nt, SUBLANES, 2 * ATT_TILE), F32),
        ],
        compiler_params=_params("arbitrary", "arbitrary"),
        name="qkv",
    )(x, g, wqt, wk, wvt)


def _causal_pairs(nt):
    return ([(q, q) for q in range(nt)]
            + [(q, k) for q in range(nt) for k in range(q)])


def _attn_kernel(qi_ref, kj_ref, lam_ref, subln_ref, qst_ref, k_ref, vt_ref, stat_ref,
                 o_ref, s0_ref, smax0_ref, p0_ref, alpha0_ref, s1_ref, smax1_ref,
                 p1_ref, alpha1_ref, m_ref, l_ref, acc_ref, *, lambda_init, nt,
                 n_pairs):
    t = ATT_TILE
    n = n_pairs
    s_ref, smax_ref = (s0_ref, s1_ref), (smax0_ref, smax1_ref)
    p_ref, alpha_ref = (p0_ref, p1_ref), (alpha0_ref, alpha1_ref)

    def scores(i, masked):
        kk = k_ref[0, 0, pl.ds(pl.multiple_of(kj_ref[i] * t, t), t), :]
        s = jnp.dot(kk, qst_ref[0, 0, qi_ref[i]],
                    preferred_element_type=F32)
        if masked:
            key = lax.broadcasted_iota(jnp.int32, (t, 2 * t), 0)
            qry = lax.broadcasted_iota(jnp.int32, (t, 2 * t), 1)
            qry = jnp.where(qry >= t, qry - t, qry)
            s = jnp.where(key <= qry, s, -jnp.inf)
        return s

    def key_sum(p):
        return jnp.sum(p.reshape(p.shape[0] // SUBLANES, SUBLANES, p.shape[1]), axis=0)

    def pv(i, slot):
        return jnp.dot(vt_ref[0, 0, kj_ref[i]], p_ref[slot][...],
                       preferred_element_type=F32)

    halves = (slice(0, t), slice(t, 2 * t))

    def diag_blocks(cols):
        c0, hb = cols.start, t // 2
        return ((hb, slice(c0, c0 + hb), 0), (t, slice(c0 + hb, c0 + t), hb))

    def fast_a(i, slot, masked, cols):
        q = qi_ref[i]
        key0 = pl.multiple_of(kj_ref[i] * t, t)
        for nk, qc, off in (diag_blocks(cols) if masked else ((t, cols, 0),)):
            w = qc.stop - qc.start
            s = jnp.dot(k_ref[0, 0, pl.ds(key0, nk), :], qst_ref[0, 0, q, :, qc],
                        preferred_element_type=F32)
            if masked:
                key = lax.broadcasted_iota(jnp.int32, (nk, w), 0)
                qry = lax.broadcasted_iota(jnp.int32, (nk, w), 1) + off
                s = jnp.where(key <= qry, s, -jnp.inf)
            p = jnp.exp2(s - m_ref[q, :, qc])
            l_ref[q, :, qc] = key_sum(p) if masked else l_ref[q, :, qc] + key_sum(p)
            p_ref[slot][:nk, qc] = p.astype(BF16)
            if masked and nk < t and i == nt - 1:
                p_ref[slot][nk:, qc] = jnp.zeros((t - nk, w), BF16)

    def fast_c(i, slot, cols, first):
        q = qi_ref[i]
        if first:
            for nk, qc, _ in diag_blocks(cols):
                acc_ref[q, :, qc] = jnp.dot(vt_ref[0, 0, kj_ref[i], :, :nk],
                                            p_ref[slot][:nk, qc],
                                            preferred_element_type=F32)
        else:
            acc_ref[q, :, cols] = acc_ref[q, :, cols] + jnp.dot(
                vt_ref[0, 0, kj_ref[i]], p_ref[slot][:, cols],
                preferred_element_type=F32)

    def fast_step(i, slot, masked):
        for cols in halves:
            fast_a(i, slot, masked, cols)
            fast_c(i - 1, 1 - slot, cols, first=masked)

    def slow_a(i, slot, masked):
        s = scores(i, masked)
        s_ref[slot][...] = s
        smax_ref[slot][...] = jnp.max(s, axis=0, keepdims=True)

    def slow_b(i, slot):
        q = qi_ref[i]
        m_prev = m_ref[q]
        m_new = jnp.maximum(m_prev, smax_ref[slot][...])
        alpha = jnp.exp2(m_prev - m_new)
        p = jnp.exp2(s_ref[slot][...] - m_new)
        l_ref[q] = alpha * l_ref[q] + key_sum(p)
        alpha_ref[slot][...] = alpha
        p_ref[slot][...] = p.astype(BF16)
        m_ref[q] = m_new

    def slow_c(i, slot):
        q = qi_ref[i]
        acc_ref[q] = alpha_ref[slot][...] * acc_ref[q] + pv(i, slot)

    def slow_step(i, slot, masked):
        slow_a(i, slot, masked)
        slow_b(i - 1, 1 - slot)
        slow_c(i - 2, slot)

    def multi_steps(step, masked, pairs_per_body):
        def body(h, carry):
            for u in range(pairs_per_body):
                step(pairs_per_body * h + u, u % 2, masked)
            return carry
        return body

    def two_steps(step, masked):
        return multi_steps(step, masked, 2)

    assert nt % FAST_UNROLL == 0 and n % FAST_UNROLL == 0 and FAST_UNROLL % 2 == 0

    def fast_path():
        for cols in halves:
            fast_a(0, 0, True, cols)
        for i in range(1, nt):
            fast_step(i, i % 2, True)
        acc_ref[nt - 1] = jnp.zeros(acc_ref.shape[1:], F32)
        lax.fori_loop(nt // FAST_UNROLL, n // FAST_UNROLL,
                      multi_steps(fast_step, False, FAST_UNROLL), 0)
        for cols in halves:
            fast_c(n - 1, 1, cols, first=False)

    def slow_path():
        m_ref[...] = jnp.full_like(m_ref, -jnp.inf)
        l_ref[...] = jnp.zeros_like(l_ref)
        acc_ref[...] = jnp.zeros_like(acc_ref)
        slow_a(0, 0, True)
        slow_a(1, 1, True)
        slow_b(0, 0)
        lax.fori_loop(1, nt // 2, two_steps(slow_step, True), 0)
        lax.fori_loop(nt // 2, n // 2, two_steps(slow_step, False), 0)
        slow_b(n - 1, 1)
        slow_c(n - 2, 0)
        slow_c(n - 1, 1)

    stat = stat_ref[0, 0]
    k_norm = jnp.max(stat[:, STAT_KNORM:STAT_KNORM + 1, :], axis=0)
    lane = lax.broadcasted_iota(jnp.int32, (1, 2 * t), 1)
    neg = jnp.full_like(k_norm, -jnp.inf)
    k_max = jnp.where(
        lane < t,
        jnp.max(jnp.where(lane < t, k_norm, neg), axis=1, keepdims=True),
        jnp.max(jnp.where(lane >= t, k_norm, neg), axis=1, keepdims=True))
    shift = stat[:, STAT_QNORM:STAT_QNORM + 1, :] * (k_max * BOUND_SLACK)
    gap = shift - stat[:, STAT_DIAG:STAT_DIAG + 1, :]
    safe = jnp.max(jnp.where(gap <= SAFE_GAP, 0.0, 1.0)) == 0.0
    m_ref[...] = shift
    lax.cond(safe, fast_path, slow_path)

    def finalize(q, carry):
        lam_p = lam_ref[...]
        lam = (jnp.exp(jnp.sum(lam_p[0:1] * lam_p[1:2], axis=1, keepdims=True))
               - jnp.exp(jnp.sum(lam_p[2:3] * lam_p[3:4], axis=1, keepdims=True))
               + lambda_init)
        acc = acc_ref[q]
        inv_sum = 1.0 / jnp.sum(l_ref[q], axis=0, keepdims=True)
        d = (acc[:, :t] * inv_sum[:, :t]
             - acc[:, t:] * (lam * inv_sum[:, t:]))
        ms = jnp.mean(d * d, axis=0, keepdims=True)
        o = d * lax.rsqrt(ms + EPS) * (subln_ref[...] * (1.0 - lambda_init))
        o_ref[0, pl.ds(pl.multiple_of(q * t, t), t), :] = o.T.astype(BF16)
        return carry

    lax.fori_loop(0, nt, finalize, 0, unroll=4)


def _attn(lam_p, subln, qst, k, vt, stat, batch, seq, lambda_init):
    nt = seq // ATT_TILE
    pairs = _causal_pairs(nt)
    qi = jnp.array([p[0] for p in pairs], jnp.int32)
    kj = jnp.array([p[1] for p in pairs], jnp.int32)
    grid_spec = pltpu.PrefetchScalarGridSpec(
        num_scalar_prefetch=2,
        grid=(batch, N_HEADS),
        in_specs=[
            _resident((4, HEAD_DIM)), _resident((V_DIM, 1)),
            pl.BlockSpec((1, 1, nt, V_DIM, 2 * ATT_TILE),
                         lambda b, h, *_: (b, h, 0, 0, 0)),
            pl.BlockSpec((1, 1, seq, V_DIM), lambda b, h, *_: (b, h, 0, 0)),
            pl.BlockSpec((1, 1, nt, V_DIM, ATT_TILE),
                         lambda b, h, *_: (b, h, 0, 0, 0)),
            pl.BlockSpec((1, 1, nt, SUBLANES, 2 * ATT_TILE),
                         lambda b, h, *_: (b, h, 0, 0, 0)),
        ],
        out_specs=pl.BlockSpec((1, seq, V_DIM), lambda b, h, *_: (b, 0, h)),
        scratch_shapes=2 * [pltpu.VMEM((ATT_TILE, 2 * ATT_TILE), F32),
                            pltpu.VMEM((1, 2 * ATT_TILE), F32),
                            pltpu.VMEM((ATT_TILE, 2 * ATT_TILE), BF16),
                            pltpu.VMEM((1, 2 * ATT_TILE), F32)]
        + [pltpu.VMEM((nt, 1, 2 * ATT_TILE), F32),
           pltpu.VMEM((nt, SUBLANES, 2 * ATT_TILE), F32),
           pltpu.VMEM((nt, V_DIM, 2 * ATT_TILE), F32)],
    )
    return pl.pallas_call(
        functools.partial(_attn_kernel, lambda_init=lambda_init, nt=nt,
                          n_pairs=len(pairs)),
        grid_spec=grid_spec,
        out_shape=jax.ShapeDtypeStruct((batch, seq, N_HEADS * V_DIM), BF16),
        compiler_params=_params("arbitrary", "arbitrary"),
        name="diff_attn",
    )(qi, kj, lam_p, subln, qst, k, vt, stat)


CONV_ROWS = 512


def _conv_kernel(x_ref, g_ref, win_ref, wc_ref, wout_ref, o_ref, tail_ref):
    t = CONV_ROWS

    @pl.when(pl.program_id(1) == 0)
    def _():
        tail_ref[...] = jnp.zeros_like(tail_ref)

    x = x_ref[...]
    xn = _rms(x, g_ref[...]).astype(BF16)
    proj = jnp.dot(xn, win_ref[...], preferred_element_type=F32)
    gb = proj[:, :D_MODEL]
    u = proj[:, D_MODEL:2 * D_MODEL] * proj[:, 2 * D_MODEL:]
    tail = tail_ref[...]
    prev1 = tail[SUBLANES - 1:SUBLANES]
    prev2 = tail[SUBLANES - 2:SUBLANES - 1]
    row = lax.broadcasted_iota(jnp.int32, (t, D_MODEL), 0)
    u1 = jnp.where(row == 0, prev1, pltpu.roll(u, 1, 0))
    u2 = jnp.where(row == 0, prev2,
                   jnp.where(row == 1, prev1, pltpu.roll(u, 2, 0)))
    wc = wc_ref[...]
    y = wc[0:1] * u2 + wc[1:2] * u1 + wc[2:3] * u
    tail_ref[...] = u[t - SUBLANES:, :]
    z = (gb * y).astype(BF16)
    o_ref[...] = x + jnp.dot(z, wout_ref[...], preferred_element_type=F32)


def _conv(x, g, w_in, w_conv, w_out, batch, seq):
    nt = seq // CONV_ROWS
    row_spec = pl.BlockSpec((CONV_ROWS, D_MODEL), lambda b, i: (b * nt + i, 0))
    return pl.pallas_call(
        _conv_kernel,
        grid=(batch, nt),
        in_specs=[row_spec, _resident((1, D_MODEL)),
                  _resident((D_MODEL, 3 * D_MODEL)),
                  _resident((CONV_WIDTH, D_MODEL)),
                  _resident((D_MODEL, D_MODEL))],
        out_specs=row_spec,
        out_shape=jax.ShapeDtypeStruct((batch * seq, D_MODEL), F32),
        scratch_shapes=[pltpu.VMEM((SUBLANES, D_MODEL), F32)],
        compiler_params=_params("arbitrary", "arbitrary"),
        name="short_conv",
    )(x, g, w_in, w_conv, w_out)


def kernel(x, ffn1_norm, ffn1_w_gate, ffn1_w_up, ffn1_w_down, mix_norm, attn_w_qkv, attn_lambda_q1, attn_lambda_k1, attn_lambda_q2, attn_lambda_k2, attn_subln, attn_w_out, conv_w_in, conv_w, conv_w_out, ffn2_norm, ffn2_w_gate, ffn2_w_up, ffn2_w_down, final_norm):
    batch, seq, d = x.shape
    depth = ffn1_norm.shape[0]
    bf = lambda w: w.astype(BF16)
    row = lambda v: v.reshape(1, -1)
    h = x.reshape(batch * seq, d)
    for i in range(depth):
        h = _ffn(h, (row(ffn1_norm[i]), bf(ffn1_w_gate[i]), bf(ffn1_w_up[i]),
                     bf(ffn1_w_down[i])))
        j = i // N_MIXERS
        proj = None
        if i % N_MIXERS == 0:
            w = attn_w_qkv[j]
            qst, k, vt, stat = _qkv(h, row(mix_norm[i]), bf(w[:, :d].T),
                                    bf(w[:, d:2 * d]), bf(w[:, 2 * d:].T), batch, seq)
            lam_p = jnp.stack([attn_lambda_q1[j], attn_lambda_k1[j],
                               attn_lambda_q2[j], attn_lambda_k2[j]])
            lambda_init = 0.8 - 0.6 * math.exp(-0.3 * i)
            a = _attn(lam_p, attn_subln[j].reshape(-1, 1), qst, k, vt, stat, batch,
                      seq, lambda_init)
            proj = (a.reshape(batch * seq, d), bf(attn_w_out[j]))
        else:
            h = _conv(h, row(mix_norm[i]), bf(conv_w_in[j]), conv_w[j],
                      bf(conv_w_out[j]), batch, seq)
        h = _ffn(h, (row(ffn2_norm[i]), bf(ffn2_w_gate[i]), bf(ffn2_w_up[i]),
                     bf(ffn2_w_down[i])), proj=proj,
                 final_g=row(final_norm) if i == depth - 1 else None)
    return h.reshape(batch, seq, d)
```

```python
import functools
import math

import jax
import jax.numpy as jnp
from jax import lax
from jax.experimental import pallas as pl
from jax.experimental.pallas import tpu as pltpu

D_MODEL = 1024
N_HEADS = 8
HEAD_DIM = 64
V_DIM = 2 * HEAD_DIM
D_FF = 2816
CONV_WIDTH = 3
EPS = 1e-5
N_MIXERS = 2

LANES = 128
SUBLANES = 8
BF16_ROWS = 16
MXU_DIM = 256
VMEM_LIMIT = 56 * 1024 * 1024

F32 = jnp.float32
BF16 = jnp.bfloat16
NT_DIMS = (((1,), (1,)), ((), ()))
LOG2_E = math.log2(math.e)


def _rms(x, g):
    ms = jnp.mean(x * x, axis=-1, keepdims=True)
    return x * lax.rsqrt(ms + EPS) * g


def _resident(shape):
    return pl.BlockSpec(shape, lambda *_: (0,) * len(shape),
                        pipeline_mode=pl.Buffered(1))


def _params(*sem, flags=None):
    return pltpu.CompilerParams(dimension_semantics=sem,
                                vmem_limit_bytes=VMEM_LIMIT, flags=flags)


FFN_ROWS = 1024
FFN_CHUNK = MXU_DIM


def _ffn_kernel(*refs, has_proj, has_final):
    refs = list(refs)
    x = refs.pop(0)[...]
    if has_proj:
        a_ref, wo_ref = refs.pop(0), refs.pop(0)
        x = x + jnp.dot(a_ref[...], wo_ref[...], preferred_element_type=F32)
    g_ref, wg_ref, wu_ref, wd_ref = refs[:4]
    o_ref, h_ref = refs[-2:]
    xn = _rms(x, g_ref[...]).astype(BF16)
    for c in range(D_FF // FFN_CHUNK):
        sl = slice(c * FFN_CHUNK, (c + 1) * FFN_CHUNK)
        gate = jnp.dot(xn, wg_ref[:, sl], preferred_element_type=F32)
        up = jnp.dot(xn, wu_ref[:, sl], preferred_element_type=F32)
        h_ref[:, sl] = (gate * jax.nn.sigmoid(gate) * up).astype(BF16)
    y = x + 0.5 * jnp.dot(h_ref[...], wd_ref[...], preferred_element_type=F32)
    if has_final:
        y = _rms(y, refs[4][...])
    o_ref[...] = y


def _ffn(x, ffn_w, proj=None, final_g=None):
    rows = x.shape[0]
    row_spec = pl.BlockSpec((FFN_ROWS, D_MODEL), lambda i: (i, 0))
    args, specs = [x], [row_spec]
    if proj is not None:
        args += list(proj)
        specs += [row_spec, _resident((D_MODEL, D_MODEL))]
    args += list(ffn_w)
    specs += [_resident((1, D_MODEL)), _resident((D_MODEL, D_FF)),
              _resident((D_MODEL, D_FF)), _resident((D_FF, D_MODEL))]
    if final_g is not None:
        args.append(final_g)
        specs.append(_resident((1, D_MODEL)))
    return pl.pallas_call(
        functools.partial(_ffn_kernel, has_proj=proj is not None,
                          has_final=final_g is not None),
        grid=(rows // FFN_ROWS,),
        in_specs=specs,
        out_specs=row_spec,
        out_shape=jax.ShapeDtypeStruct((rows, D_MODEL), F32),
        scratch_shapes=[pltpu.VMEM((FFN_ROWS, D_FF), BF16)],
        compiler_params=_params("arbitrary"),
        name="ffn",
    )(*args)


ATT_TILE = 512
STAT_QNORM, STAT_DIAG, STAT_KNORM, STAT_ROWS_USED = 0, 1, 2, 3
BOUND_SLACK = 1.001
SAFE_GAP = 64.0
FAST_UNROLL = 8


def _qkv_kernel(x_ref, g_ref, wqt_ref, wk_ref, wvt_ref, qst_ref, k_ref, vt_ref,
                stat_ref):
    t = ATT_TILE
    xn = _rms(x_ref[...], g_ref[...]).astype(BF16)
    qt = lax.dot_general(wqt_ref[...], xn, NT_DIMS, preferred_element_type=F32)
    qt = (qt * (HEAD_DIM ** -0.5 * LOG2_E)).astype(BF16)
    k = jnp.dot(xn, wk_ref[...], preferred_element_type=F32).astype(BF16)
    vt = lax.dot_general(wvt_ref[...], xn, NT_DIMS,
                         preferred_element_type=F32).astype(BF16)
    feat = lax.broadcasted_iota(jnp.int32, (V_DIM, t), 0)
    zero = jnp.zeros((V_DIM, t), BF16)
    qf = qt.astype(F32)
    kf = k.astype(F32).T
    col_norm = lambda a: jnp.sqrt(jnp.sum(a * a, axis=0, keepdims=True))
    for h in range(N_HEADS):
        sl = slice(h * V_DIM, (h + 1) * V_DIM)
        for c in range(2):
            rows = slice(h * V_DIM + c * HEAD_DIM, h * V_DIM + (c + 1) * HEAD_DIM)
            cols = slice(c * t, (c + 1) * t)
            q_c, k_c = qf[rows, :], kf[rows, :]
            stat_ref[0, h, 0, STAT_QNORM:STAT_QNORM + 1, cols] = col_norm(q_c)
            stat_ref[0, h, 0, STAT_DIAG:STAT_DIAG + 1, cols] = jnp.sum(
                q_c * k_c, axis=0, keepdims=True)
            stat_ref[0, h, 0, STAT_KNORM:STAT_KNORM + 1, cols] = col_norm(k_c)
        stat_ref[0, h, 0, STAT_ROWS_USED:, :] = jnp.zeros(
            (SUBLANES - STAT_ROWS_USED, 2 * t), F32)
        qst_ref[0, h, 0, :, :t] = jnp.where(feat < HEAD_DIM, qt[sl, :], zero)
        qst_ref[0, h, 0, :, t:] = jnp.where(feat >= HEAD_DIM, qt[sl, :], zero)
        k_ref[0, h] = k[:, sl]
        vt_ref[0, h, 0] = vt[sl, :]


def _qkv(x, g, wqt, wk, wvt, batch, seq):
    nt = seq // ATT_TILE
    return pl.pallas_call(
        _qkv_kernel,
        grid=(batch, nt),
        in_specs=[pl.BlockSpec((ATT_TILE, D_MODEL), lambda b, i: (b * nt + i, 0)),
                  _resident((1, D_MODEL)), _resident((D_MODEL, D_MODEL)),
                  _resident((D_MODEL, D_MODEL)), _resident((D_MODEL, D_MODEL))],
        out_specs=[
            pl.BlockSpec((1, N_HEADS, 1, V_DIM, 2 * ATT_TILE),
                         lambda b, i: (b, 0, i, 0, 0)),
            pl.BlockSpec((1, N_HEADS, ATT_TILE, V_DIM), lambda b, i: (b, 0, i, 0)),
            pl.BlockSpec((1, N_HEADS, 1, V_DIM, ATT_TILE),
                         lambda b, i: (b, 0, i, 0, 0)),
            pl.BlockSpec((1, N_HEADS, 1, SUBLANES, 2 * ATT_TILE),
                         lambda b, i: (b, 0, i, 0, 0)),
        ],
        out_shape=[
            jax.ShapeDtypeStruct((batch, N_HEADS, nt, V_DIM, 2 * ATT_TILE), BF16),
            jax.ShapeDtypeStruct((batch, N_HEADS, seq, V_DIM), BF16),
            jax.ShapeDtypeStruct((batch, N_HEADS, nt, V_DIM, ATT_TILE), BF16),
            jax.ShapeDtypeStruct((batch, N_HEADS, nt, SUBLANES, 2 * ATT_TILE), F32),
        ],
        compiler_params=_params("arbitrary", "arbitrary"),
        name="qkv",
    )(x, g, wqt, wk, wvt)


def _causal_pairs(nt):
    return ([(q, q) for q in range(nt)]
            + [(q, k) for q in range(nt) for k in range(q)])


def _attn_kernel(qi_ref, kj_ref, lam_ref, subln_ref, qst_ref, k_ref, vt_ref, stat_ref,
                 o_ref, s0_ref, smax0_ref, p0_ref, alpha0_ref, s1_ref, smax1_ref,
                 p1_ref, alpha1_ref, m_ref, l_ref, acc_ref, *, lambda_init, nt,
                 n_pairs):
    t = ATT_TILE
    n = n_pairs
    s_ref, smax_ref = (s0_ref, s1_ref), (smax0_ref, smax1_ref)
    p_ref, alpha_ref = (p0_ref, p1_ref), (alpha0_ref, alpha1_ref)

    def scores(i, masked):
        kk = k_ref[0, 0, pl.ds(pl.multiple_of(kj_ref[i] * t, t), t), :]
        s = jnp.dot(kk, qst_ref[0, 0, qi_ref[i]],
                    preferred_element_type=F32)
        if masked:
            key = lax.broadcasted_iota(jnp.int32, (t, 2 * t), 0)
            qry = lax.broadcasted_iota(jnp.int32, (t, 2 * t), 1)
            qry = jnp.where(qry >= t, qry - t, qry)
            s = jnp.where(key <= qry, s, -jnp.inf)
        return s

    def key_sum(p):
        return jnp.sum(p.reshape(p.shape[0] // SUBLANES, SUBLANES, p.shape[1]), axis=0)

    def pv(i, slot):
        return jnp.dot(vt_ref[0, 0, kj_ref[i]], p_ref[slot][...],
                       preferred_element_type=F32)

    halves = (slice(0, t), slice(t, 2 * t))

    def diag_blocks(cols):
        c0, hb = cols.start, t // 2
        return ((hb, slice(c0, c0 + hb), 0), (t, slice(c0 + hb, c0 + t), hb))

    def fast_a(i, slot, masked, cols):
        q = qi_ref[i]
        key0 = pl.multiple_of(kj_ref[i] * t, t)
        for nk, qc, off in (diag_blocks(cols) if masked else ((t, cols, 0),)):
            w = qc.stop - qc.start
            s = jnp.dot(k_ref[0, 0, pl.ds(key0, nk), :], qst_ref[0, 0, q, :, qc],
                        preferred_element_type=F32)
            if masked:
                key = lax.broadcasted_iota(jnp.int32, (nk, w), 0)
                qry = lax.broadcasted_iota(jnp.int32, (nk, w), 1) + off
                s = jnp.where(key <= qry, s, -jnp.inf)
            p = jnp.exp2(s - m_ref[q, :, qc])
            l_ref[q, :, qc] = key_sum(p) if masked else l_ref[q, :, qc] + key_sum(p)
            p_ref[slot][:nk, qc] = p.astype(BF16)
            if masked and nk < t and i == nt - 1:
                p_ref[slot][nk:, qc] = jnp.zeros((t - nk, w), BF16)

    def fast_c(i, slot, cols, first):
        q = qi_ref[i]
        if first:
            for nk, qc, _ in diag_blocks(cols):
                acc_ref[q, :, qc] = jnp.dot(vt_ref[0, 0, kj_ref[i], :, :nk],
                                            p_ref[slot][:nk, qc],
                                            preferred_element_type=F32)
        else:
            acc_ref[q, :, cols] = acc_ref[q, :, cols] + jnp.dot(
                vt_ref[0, 0, kj_ref[i]], p_ref[slot][:, cols],
                preferred_element_type=F32)

    def fast_step(i, slot, masked):
        for cols in halves:
            fast_a(i, slot, masked, cols)
            fast_c(i - 1, 1 - slot, cols, first=masked)

    def slow_a(i, slot, masked):
        s = scores(i, masked)
        s_ref[slot][...] = s
        smax_ref[slot][...] = jnp.max(s, axis=0, keepdims=True)

    def slow_b(i, slot):
        q = qi_ref[i]
        m_prev = m_ref[q]
        m_new = jnp.maximum(m_prev, smax_ref[slot][...])
        alpha = jnp.exp2(m_prev - m_new)
        p = jnp.exp2(s_ref[slot][...] - m_new)
        l_ref[q] = alpha * l_ref[q] + key_sum(p)
        alpha_ref[slot][...] = alpha
        p_ref[slot][...] = p.astype(BF16)
        m_ref[q] = m_new

    def slow_c(i, slot):
        q = qi_ref[i]
        acc_ref[q] = alpha_ref[slot][...] * acc_ref[q] + pv(i, slot)

    def slow_step(i, slot, masked):
        slow_a(i, slot, masked)
        slow_b(i - 1, 1 - slot)
        slow_c(i - 2, slot)

    def multi_steps(step, masked, pairs_per_body):
        def body(h, carry):
            for u in range(pairs_per_body):
                step(pairs_per_body * h + u, u % 2, masked)
            return carry
        return body

    def two_steps(step, masked):
        return multi_steps(step, masked, 2)

    assert nt % FAST_UNROLL == 0 and n % FAST_UNROLL == 0 and FAST_UNROLL % 2 == 0

    def fast_path():
        for cols in halves:
            fast_a(0, 0, True, cols)
        for i in range(1, nt):
            fast_step(i, i % 2, True)
        acc_ref[nt - 1] = jnp.zeros(acc_ref.shape[1:], F32)
        lax.fori_loop(nt // FAST_UNROLL, n // FAST_UNROLL,
                      multi_steps(fast_step, False, FAST_UNROLL), 0)
        for cols in halves:
            fast_c(n - 1, 1, cols, first=False)

    def slow_path():
        m_ref[...] = jnp.full_like(m_ref, -jnp.inf)
        l_ref[...] = jnp.zeros_like(l_ref)
        acc_ref[...] = jnp.zeros_like(acc_ref)
        slow_a(0, 0, True)
        slow_a(1, 1, True)
        slow_b(0, 0)
        lax.fori_loop(1, nt // 2, two_steps(slow_step, True), 0)
        lax.fori_loop(nt // 2, n // 2, two_steps(slow_step, False), 0)
        slow_b(n - 1, 1)
        slow_c(n - 2, 0)
        slow_c(n - 1, 1)

    stat = stat_ref[0, 0]
    k_norm = jnp.max(stat[:, STAT_KNORM:STAT_KNORM + 1, :], axis=0)
    lane = lax.broadcasted_iota(jnp.int32, (1, 2 * t), 1)
    neg = jnp.full_like(k_norm, -jnp.inf)
    k_max = jnp.where(
        lane < t,
        jnp.max(jnp.where(lane < t, k_norm, neg), axis=1, keepdims=True),
        jnp.max(jnp.where(lane >= t, k_norm, neg), axis=1, keepdims=True))
    shift = stat[:, STAT_QNORM:STAT_QNORM + 1, :] * (k_max * BOUND_SLACK)
    gap = shift - stat[:, STAT_DIAG:STAT_DIAG + 1, :]
    safe = jnp.max(jnp.where(gap <= SAFE_GAP, 0.0, 1.0)) == 0.0
    m_ref[...] = shift
    lax.cond(safe, fast_path, slow_path)

    def finalize(q, carry):
        lam_p = lam_ref[...]
        lam = (jnp.exp(jnp.sum(lam_p[0:1] * lam_p[1:2], axis=1, keepdims=True))
               - jnp.exp(jnp.sum(lam_p[2:3] * lam_p[3:4], axis=1, keepdims=True))
               + lambda_init)
        acc = acc_ref[q]
        inv_sum = 1.0 / jnp.sum(l_ref[q], axis=0, keepdims=True)
        d = (acc[:, :t] * inv_sum[:, :t]
             - acc[:, t:] * (lam * inv_sum[:, t:]))
        ms = jnp.mean(d * d, axis=0, keepdims=True)
        o = d * lax.rsqrt(ms + EPS) * (subln_ref[...] * (1.0 - lambda_init))
        o_ref[0, pl.ds(pl.multiple_of(q * t, t), t), :] = o.T.astype(BF16)
        return carry

    lax.fori_loop(0, nt, finalize, 0, unroll=4)


def _attn(lam_p, subln, qst, k, vt, stat, batch, seq, lambda_init):
    nt = seq // ATT_TILE
    pairs = _causal_pairs(nt)
    qi = jnp.array([p[0] for p in pairs], jnp.int32)
    kj = jnp.array([p[1] for p in pairs], jnp.int32)
    grid_spec = pltpu.PrefetchScalarGridSpec(
        num_scalar_prefetch=2,
        grid=(batch, N_HEADS),
        in_specs=[
            _resident((4, HEAD_DIM)), _resident((V_DIM, 1)),
            pl.BlockSpec((1, 1, nt, V_DIM, 2 * ATT_TILE),
                         lambda b, h, *_: (b, h, 0, 0, 0)),
            pl.BlockSpec((1, 1, seq, V_DIM), lambda b, h, *_: (b, h, 0, 0)),
            pl.BlockSpec((1, 1, nt, V_DIM, ATT_TILE),
                         lambda b, h, *_: (b, h, 0, 0, 0)),
            pl.BlockSpec((1, 1, nt, SUBLANES, 2 * ATT_TILE),
                         lambda b, h, *_: (b, h, 0, 0, 0)),
        ],
        out_specs=pl.BlockSpec((1, seq, V_DIM), lambda b, h, *_: (b, 0, h)),
        scratch_shapes=2 * [pltpu.VMEM((ATT_TILE, 2 * ATT_TILE), F32),
                            pltpu.VMEM((1, 2 * ATT_TILE), F32),
                            pltpu.VMEM((ATT_TILE, 2 * ATT_TILE), BF16),
                            pltpu.VMEM((1, 2 * ATT_TILE), F32)]
        + [pltpu.VMEM((nt, 1, 2 * ATT_TILE), F32),
           pltpu.VMEM((nt, SUBLANES, 2 * ATT_TILE), F32),
           pltpu.VMEM((nt, V_DIM, 2 * ATT_TILE), F32)],
    )
    return pl.pallas_call(
        functools.partial(_attn_kernel, lambda_init=lambda_init, nt=nt,
                          n_pairs=len(pairs)),
        grid_spec=grid_spec,
        out_shape=jax.ShapeDtypeStruct((batch, seq, N_HEADS * V_DIM), BF16),
        compiler_params=_params("arbitrary", "arbitrary"),
        name="diff_attn",
    )(qi, kj, lam_p, subln, qst, k, vt, stat)


CONV_ROWS = 1024
CONV_CHUNK = MXU_DIM


def _conv_kernel(x_ref, g_ref, win_ref, wc_ref, wout_ref, o_ref, tail_ref, z_ref):
    t = CONV_ROWS

    @pl.when(pl.program_id(1) == 0)
    def _():
        tail_ref[...] = jnp.zeros_like(tail_ref)

    x = x_ref[...]
    xn = _rms(x, g_ref[...]).astype(BF16)
    row = lax.broadcasted_iota(jnp.int32, (t, CONV_CHUNK), 0)
    for c in range(D_MODEL // CONV_CHUNK):
        cols = slice(c * CONV_CHUNK, (c + 1) * CONV_CHUNK)
        part = lambda k: jnp.dot(
            xn, win_ref[:, k * D_MODEL + cols.start:k * D_MODEL + cols.stop],
            preferred_element_type=F32)
        gb = part(0)
        u = part(1) * part(2)
        tail = tail_ref[:, cols]
        prev1 = tail[SUBLANES - 1:SUBLANES]
        prev2 = tail[SUBLANES - 2:SUBLANES - 1]
        u1 = jnp.where(row == 0, prev1, pltpu.roll(u, 1, 0))
        u2 = jnp.where(row == 0, prev2,
                       jnp.where(row == 1, prev1, pltpu.roll(u, 2, 0)))
        wc = wc_ref[:, cols]
        y = wc[0:1] * u2 + wc[1:2] * u1 + wc[2:3] * u
        tail_ref[:, cols] = u[t - SUBLANES:, :]
        z_ref[:, cols] = (gb * y).astype(BF16)
    o_ref[...] = x + jnp.dot(z_ref[...], wout_ref[...], preferred_element_type=F32)


def _conv(x, g, w_in, w_conv, w_out, batch, seq):
    nt = seq // CONV_ROWS
    row_spec = pl.BlockSpec((CONV_ROWS, D_MODEL), lambda b, i: (b * nt + i, 0))
    return pl.pallas_call(
        _conv_kernel,
        grid=(batch, nt),
        in_specs=[row_spec, _resident((1, D_MODEL)),
                  _resident((D_MODEL, 3 * D_MODEL)),
                  _resident((CONV_WIDTH, D_MODEL)),
                  _resident((D_MODEL, D_MODEL))],
        out_specs=row_spec,
        out_shape=jax.ShapeDtypeStruct((batch * seq, D_MODEL), F32),
        scratch_shapes=[pltpu.VMEM((SUBLANES, D_MODEL), F32),
                        pltpu.VMEM((CONV_ROWS, D_MODEL), BF16)],
        compiler_params=_params("arbitrary", "arbitrary"),
        name="short_conv",
    )(x, g, w_in, w_conv, w_out)


def kernel(x, ffn1_norm, ffn1_w_gate, ffn1_w_up, ffn1_w_down, mix_norm, attn_w_qkv, attn_lambda_q1, attn_lambda_k1, attn_lambda_q2, attn_lambda_k2, attn_subln, attn_w_out, conv_w_in, conv_w, conv_w_out, ffn2_norm, ffn2_w_gate, ffn2_w_up, ffn2_w_down, final_norm):
    batch, seq, d = x.shape
    depth = ffn1_norm.shape[0]
    bf = lambda w: w.astype(BF16)
    row = lambda v: v.reshape(1, -1)
    h = x.reshape(batch * seq, d)
    for i in range(depth):
        h = _ffn(h, (row(ffn1_norm[i]), bf(ffn1_w_gate[i]), bf(ffn1_w_up[i]),
                     bf(ffn1_w_down[i])))
        j = i // N_MIXERS
        proj = None
        if i % N_MIXERS == 0:
            w = attn_w_qkv[j]
            qst, k, vt, stat = _qkv(h, row(mix_norm[i]), bf(w[:, :d].T),
                                    bf(w[:, d:2 * d]), bf(w[:, 2 * d:].T), batch, seq)
            lam_p = jnp.stack([attn_lambda_q1[j], attn_lambda_k1[j],
                               attn_lambda_q2[j], attn_lambda_k2[j]])
            lambda_init = 0.8 - 0.6 * math.exp(-0.3 * i)
            a = _attn(lam_p, attn_subln[j].reshape(-1, 1), qst, k, vt, stat, batch,
                      seq, lambda_init)
            proj = (a.reshape(batch * seq, d), bf(attn_w_out[j]))
        else:
            h = _conv(h, row(mix_norm[i]), bf(conv_w_in[j]), conv_w[j],
                      bf(conv_w_out[j]), batch, seq)
        h = _ffn(h, (row(ffn2_norm[i]), bf(ffn2_w_gate[i]), bf(ffn2_w_up[i]),
                     bf(ffn2_w_down[i])), proj=proj,
                 final_g=row(final_norm) if i == depth - 1 else None)
    return h.reshape(batch, seq, d)
```

```python
import functools
import math

import jax
import jax.numpy as jnp
from jax import lax
from jax.experimental import pallas as pl
from jax.experimental.pallas import tpu as pltpu

D_MODEL = 1024
N_HEADS = 8
HEAD_DIM = 64
V_DIM = 2 * HEAD_DIM
D_FF = 2816
CONV_WIDTH = 3
EPS = 1e-5
N_MIXERS = 2

LANES = 128
SUBLANES = 8
BF16_ROWS = 16
MXU_DIM = 256
VMEM_LIMIT = 56 * 1024 * 1024

F32 = jnp.float32
BF16 = jnp.bfloat16
NT_DIMS = (((1,), (1,)), ((), ()))
LOG2_E = math.log2(math.e)


def _rms(x, g):
    ms = jnp.mean(x * x, axis=-1, keepdims=True)
    return x * lax.rsqrt(ms + EPS) * g


def _resident(shape):
    return pl.BlockSpec(shape, lambda *_: (0,) * len(shape),
                        pipeline_mode=pl.Buffered(1))


def _params(*sem, flags=None):
    return pltpu.CompilerParams(dimension_semantics=sem,
                                vmem_limit_bytes=VMEM_LIMIT, flags=flags)


FFN_ROWS = 1024
FFN_CHUNK = MXU_DIM


def _ffn_kernel(*refs, has_proj, has_final):
    refs = list(refs)
    x = refs.pop(0)[...]
    if has_proj:
        a_ref, wo_ref = refs.pop(0), refs.pop(0)
        x = x + jnp.dot(a_ref[...], wo_ref[...], preferred_element_type=F32)
    g_ref, wg_ref, wu_ref, wd_ref = refs[:4]
    o_ref, h_ref = refs[-2:]
    xn = _rms(x, g_ref[...]).astype(BF16)
    for c in range(D_FF // FFN_CHUNK):
        sl = slice(c * FFN_CHUNK, (c + 1) * FFN_CHUNK)
        gate = jnp.dot(xn, wg_ref[:, sl], preferred_element_type=F32)
        up = jnp.dot(xn, wu_ref[:, sl], preferred_element_type=F32)
        h_ref[:, sl] = (gate * jax.nn.sigmoid(gate) * up).astype(BF16)
    y = x + 0.5 * jnp.dot(h_ref[...], wd_ref[...], preferred_element_type=F32)
    if has_final:
        y = _rms(y, refs[4][...])
    o_ref[...] = y


def _ffn(x, ffn_w, proj=None, final_g=None):
    rows = x.shape[0]
    row_spec = pl.BlockSpec((FFN_ROWS, D_MODEL), lambda i: (i, 0))
    args, specs = [x], [row_spec]
    if proj is not None:
        args += list(proj)
        specs += [row_spec, _resident((D_MODEL, D_MODEL))]
    args += list(ffn_w)
    specs += [_resident((1, D_MODEL)), _resident((D_MODEL, D_FF)),
              _resident((D_MODEL, D_FF)), _resident((D_FF, D_MODEL))]
    if final_g is not None:
        args.append(final_g)
        specs.append(_resident((1, D_MODEL)))
    return pl.pallas_call(
        functools.partial(_ffn_kernel, has_proj=proj is not None,
                          has_final=final_g is not None),
        grid=(rows // FFN_ROWS,),
        in_specs=specs,
        out_specs=row_spec,
        out_shape=jax.ShapeDtypeStruct((rows, D_MODEL), F32),
        scratch_shapes=[pltpu.VMEM((FFN_ROWS, D_FF), BF16)],
        compiler_params=_params("arbitrary"),
        name="ffn",
    )(*args)


ATT_TILE = 512
STAT_QNORM, STAT_DIAG, STAT_KNORM, STAT_ROWS_USED = 0, 1, 2, 3
BOUND_SLACK = 1.001
SAFE_GAP = 64.0
FAST_UNROLL = 24


def _qkv_kernel(x_ref, g_ref, wqt_ref, wk_ref, wvt_ref, qst_ref, k_ref, vt_ref,
                stat_ref):
    t = ATT_TILE
    xn = _rms(x_ref[...], g_ref[...]).astype(BF16)
    qt = lax.dot_general(wqt_ref[...], xn, NT_DIMS, preferred_element_type=F32)
    qt = (qt * (HEAD_DIM ** -0.5 * LOG2_E)).astype(BF16)
    k = jnp.dot(xn, wk_ref[...], preferred_element_type=F32).astype(BF16)
    vt = lax.dot_general(wvt_ref[...], xn, NT_DIMS,
                         preferred_element_type=F32).astype(BF16)
    feat = lax.broadcasted_iota(jnp.int32, (V_DIM, t), 0)
    zero = jnp.zeros((V_DIM, t), BF16)
    qf = qt.astype(F32)
    kf = k.astype(F32).T
    col_norm = lambda a: jnp.sqrt(jnp.sum(a * a, axis=0, keepdims=True))
    for h in range(N_HEADS):
        sl = slice(h * V_DIM, (h + 1) * V_DIM)
        for c in range(2):
            rows = slice(h * V_DIM + c * HEAD_DIM, h * V_DIM + (c + 1) * HEAD_DIM)
            cols = slice(c * t, (c + 1) * t)
            q_c, k_c = qf[rows, :], kf[rows, :]
            stat_ref[0, h, 0, STAT_QNORM:STAT_QNORM + 1, cols] = col_norm(q_c)
            stat_ref[0, h, 0, STAT_DIAG:STAT_DIAG + 1, cols] = jnp.sum(
                q_c * k_c, axis=0, keepdims=True)
            stat_ref[0, h, 0, STAT_KNORM:STAT_KNORM + 1, cols] = col_norm(k_c)
        stat_ref[0, h, 0, STAT_ROWS_USED:, :] = jnp.zeros(
            (SUBLANES - STAT_ROWS_USED, 2 * t), F32)
        qst_ref[0, h, 0, :, :t] = jnp.where(feat < HEAD_DIM, qt[sl, :], zero)
        qst_ref[0, h, 0, :, t:] = jnp.where(feat >= HEAD_DIM, qt[sl, :], zero)
        k_ref[0, h] = k[:, sl]
        vt_ref[0, h, 0] = vt[sl, :]


def _qkv(x, g, wqt, wk, wvt, batch, seq):
    nt = seq // ATT_TILE
    return pl.pallas_call(
        _qkv_kernel,
        grid=(batch, nt),
        in_specs=[pl.BlockSpec((ATT_TILE, D_MODEL), lambda b, i: (b * nt + i, 0)),
                  _resident((1, D_MODEL)), _resident((D_MODEL, D_MODEL)),
                  _resident((D_MODEL, D_MODEL)), _resident((D_MODEL, D_MODEL))],
        out_specs=[
            pl.BlockSpec((1, N_HEADS, 1, V_DIM, 2 * ATT_TILE),
                         lambda b, i: (b, 0, i, 0, 0)),
            pl.BlockSpec((1, N_HEADS, ATT_TILE, V_DIM), lambda b, i: (b, 0, i, 0)),
            pl.BlockSpec((1, N_HEADS, 1, V_DIM, ATT_TILE),
                         lambda b, i: (b, 0, i, 0, 0)),
            pl.BlockSpec((1, N_HEADS, 1, SUBLANES, 2 * ATT_TILE),
                         lambda b, i: (b, 0, i, 0, 0)),
        ],
        out_shape=[
            jax.ShapeDtypeStruct((batch, N_HEADS, nt, V_DIM, 2 * ATT_TILE), BF16),
            jax.ShapeDtypeStruct((batch, N_HEADS, seq, V_DIM), BF16),
            jax.ShapeDtypeStruct((batch, N_HEADS, nt, V_DIM, ATT_TILE), BF16),
            jax.ShapeDtypeStruct((batch, N_HEADS, nt, SUBLANES, 2 * ATT_TILE), F32),
        ],
        compiler_params=_params("arbitrary", "arbitrary"),
        name="qkv",
    )(x, g, wqt, wk, wvt)


def _causal_pairs(nt):
    return ([(q, q) for q in range(nt)]
            + [(q, k) for q in range(nt) for k in range(q)])


def _attn_kernel(qi_ref, kj_ref, lam_ref, subln_ref, qst_ref, k_ref, vt_ref, stat_ref,
                 o_ref, s0_ref, smax0_ref, p0_ref, alpha0_ref, s1_ref, smax1_ref,
                 p1_ref, alpha1_ref, m_ref, l_ref, acc_ref, *, lambda_init, nt,
                 n_pairs):
    t = ATT_TILE
    n = n_pairs
    s_ref, smax_ref = (s0_ref, s1_ref), (smax0_ref, smax1_ref)
    p_ref, alpha_ref = (p0_ref, p1_ref), (alpha0_ref, alpha1_ref)

    def scores(i, masked):
        kk = k_ref[0, 0, pl.ds(pl.multiple_of(kj_ref[i] * t, t), t), :]
        s = jnp.dot(kk, qst_ref[0, 0, qi_ref[i]],
                    preferred_element_type=F32)
        if masked:
            key = lax.broadcasted_iota(jnp.int32, (t, 2 * t), 0)
            qry = lax.broadcasted_iota(jnp.int32, (t, 2 * t), 1)
            qry = jnp.where(qry >= t, qry - t, qry)
            s = jnp.where(key <= qry, s, -jnp.inf)
        return s

    def key_sum(p):
        return jnp.sum(p.reshape(p.shape[0] // SUBLANES, SUBLANES, p.shape[1]), axis=0)

    def pv(i, slot):
        return jnp.dot(vt_ref[0, 0, kj_ref[i]], p_ref[slot][...],
                       preferred_element_type=F32)

    halves = (slice(0, t), slice(t, 2 * t))

    def diag_blocks(cols):
        c0, hb = cols.start, t // 2
        return ((hb, slice(c0, c0 + hb), 0), (t, slice(c0 + hb, c0 + t), hb))

    def fast_a(i, slot, masked, cols):
        q = qi_ref[i]
        key0 = pl.multiple_of(kj_ref[i] * t, t)
        for nk, qc, off in (diag_blocks(cols) if masked else ((t, cols, 0),)):
            w = qc.stop - qc.start
            s = jnp.dot(k_ref[0, 0, pl.ds(key0, nk), :], qst_ref[0, 0, q, :, qc],
                        preferred_element_type=F32)
            if masked:
                key = lax.broadcasted_iota(jnp.int32, (nk, w), 0)
                qry = lax.broadcasted_iota(jnp.int32, (nk, w), 1) + off
                s = jnp.where(key <= qry, s, -jnp.inf)
            p = jnp.exp2(s - m_ref[q, :, qc])
            l_ref[q, :, qc] = key_sum(p) if masked else l_ref[q, :, qc] + key_sum(p)
            p_ref[slot][:nk, qc] = p.astype(BF16)
            if masked and nk < t and i == nt - 1:
                p_ref[slot][nk:, qc] = jnp.zeros((t - nk, w), BF16)

    def fast_c(i, slot, cols, first):
        q = qi_ref[i]
        if first:
            for nk, qc, _ in diag_blocks(cols):
                acc_ref[q, :, qc] = jnp.dot(vt_ref[0, 0, kj_ref[i], :, :nk],
                                            p_ref[slot][:nk, qc],
                                            preferred_element_type=F32)
        else:
            acc_ref[q, :, cols] = acc_ref[q, :, cols] + jnp.dot(
                vt_ref[0, 0, kj_ref[i]], p_ref[slot][:, cols],
                preferred_element_type=F32)

    def fast_step(i, slot, masked):
        for cols in halves:
            fast_a(i, slot, masked, cols)
            fast_c(i - 1, 1 - slot, cols, first=masked)

    def slow_a(i, slot, masked):
        s = scores(i, masked)
        s_ref[slot][...] = s
        smax_ref[slot][...] = jnp.max(s, axis=0, keepdims=True)

    def slow_b(i, slot):
        q = qi_ref[i]
        m_prev = m_ref[q]
        m_new = jnp.maximum(m_prev, smax_ref[slot][...])
        alpha = jnp.exp2(m_prev - m_new)
        p = jnp.exp2(s_ref[slot][...] - m_new)
        l_ref[q] = alpha * l_ref[q] + key_sum(p)
        alpha_ref[slot][...] = alpha
        p_ref[slot][...] = p.astype(BF16)
        m_ref[q] = m_new

    def slow_c(i, slot):
        q = qi_ref[i]
        acc_ref[q] = alpha_ref[slot][...] * acc_ref[q] + pv(i, slot)

    def slow_step(i, slot, masked):
        slow_a(i, slot, masked)
        slow_b(i - 1, 1 - slot)
        slow_c(i - 2, slot)

    def multi_steps(step, masked, pairs_per_body, first_pair=0):
        def body(h, carry):
            for u in range(pairs_per_body):
                step(first_pair + pairs_per_body * h + u, u % 2, masked)
            return carry
        return body

    def two_steps(step, masked):
        return multi_steps(step, masked, 2)

    assert nt % 2 == 0 and (n - nt) % FAST_UNROLL == 0 and FAST_UNROLL % 2 == 0

    def fast_path():
        for cols in halves:
            fast_a(0, 0, True, cols)
        for i in range(1, nt):
            fast_step(i, i % 2, True)
        acc_ref[nt - 1] = jnp.zeros(acc_ref.shape[1:], F32)
        lax.fori_loop(0, (n - nt) // FAST_UNROLL,
                      multi_steps(fast_step, False, FAST_UNROLL, first_pair=nt), 0)
        for cols in halves:
            fast_c(n - 1, 1, cols, first=False)

    def slow_path():
        m_ref[...] = jnp.full_like(m_ref, -jnp.inf)
        l_ref[...] = jnp.zeros_like(l_ref)
        acc_ref[...] = jnp.zeros_like(acc_ref)
        slow_a(0, 0, True)
        slow_a(1, 1, True)
        slow_b(0, 0)
        lax.fori_loop(1, nt // 2, two_steps(slow_step, True), 0)
        lax.fori_loop(nt // 2, n // 2, two_steps(slow_step, False), 0)
        slow_b(n - 1, 1)
        slow_c(n - 2, 0)
        slow_c(n - 1, 1)

    stat = stat_ref[0, 0]
    k_norm = jnp.max(stat[:, STAT_KNORM:STAT_KNORM + 1, :], axis=0)
    lane = lax.broadcasted_iota(jnp.int32, (1, 2 * t), 1)
    neg = jnp.full_like(k_norm, -jnp.inf)
    k_max = jnp.where(
        lane < t,
        jnp.max(jnp.where(lane < t, k_norm, neg), axis=1, keepdims=True),
        jnp.max(jnp.where(lane >= t, k_norm, neg), axis=1, keepdims=True))
    shift = stat[:, STAT_QNORM:STAT_QNORM + 1, :] * (k_max * BOUND_SLACK)
    gap = shift - stat[:, STAT_DIAG:STAT_DIAG + 1, :]
    safe = jnp.max(jnp.where(gap <= SAFE_GAP, 0.0, 1.0)) == 0.0
    m_ref[...] = shift
    lax.cond(safe, fast_path, slow_path)

    def finalize(q, carry):
        lam_p = lam_ref[...]
        lam = (jnp.exp(jnp.sum(lam_p[0:1] * lam_p[1:2], axis=1, keepdims=True))
               - jnp.exp(jnp.sum(lam_p[2:3] * lam_p[3:4], axis=1, keepdims=True))
               + lambda_init)
        acc = acc_ref[q]
        inv_sum = 1.0 / jnp.sum(l_ref[q], axis=0, keepdims=True)
        d = (acc[:, :t] * inv_sum[:, :t]
             - acc[:, t:] * (lam * inv_sum[:, t:]))
        ms = jnp.mean(d * d, axis=0, keepdims=True)
        o = d * lax.rsqrt(ms + EPS) * (subln_ref[...] * (1.0 - lambda_init))
        o_ref[0, pl.ds(pl.multiple_of(q * t, t), t), :] = o.T.astype(BF16)
        return carry

    lax.fori_loop(0, nt, finalize, 0, unroll=4)


def _attn(lam_p, subln, qst, k, vt, stat, batch, seq, lambda_init):
    nt = seq // ATT_TILE
    pairs = _causal_pairs(nt)
    qi = jnp.array([p[0] for p in pairs], jnp.int32)
    kj = jnp.array([p[1] for p in pairs], jnp.int32)
    grid_spec = pltpu.PrefetchScalarGridSpec(
        num_scalar_prefetch=2,
        grid=(batch, N_HEADS),
        in_specs=[
            _resident((4, HEAD_DIM)), _resident((V_DIM, 1)),
            pl.BlockSpec((1, 1, nt, V_DIM, 2 * ATT_TILE),
                         lambda b, h, *_: (b, h, 0, 0, 0)),
            pl.BlockSpec((1, 1, seq, V_DIM), lambda b, h, *_: (b, h, 0, 0)),
            pl.BlockSpec((1, 1, nt, V_DIM, ATT_TILE),
                         lambda b, h, *_: (b, h, 0, 0, 0)),
            pl.BlockSpec((1, 1, nt, SUBLANES, 2 * ATT_TILE),
                         lambda b, h, *_: (b, h, 0, 0, 0)),
        ],
        out_specs=pl.BlockSpec((1, seq, V_DIM), lambda b, h, *_: (b, 0, h)),
        scratch_shapes=2 * [pltpu.VMEM((ATT_TILE, 2 * ATT_TILE), F32),
                            pltpu.VMEM((1, 2 * ATT_TILE), F32),
                            pltpu.VMEM((ATT_TILE, 2 * ATT_TILE), BF16),
                            pltpu.VMEM((1, 2 * ATT_TILE), F32)]
        + [pltpu.VMEM((nt, 1, 2 * ATT_TILE), F32),
           pltpu.VMEM((nt, SUBLANES, 2 * ATT_TILE), F32),
           pltpu.VMEM((nt, V_DIM, 2 * ATT_TILE), F32)],
    )
    return pl.pallas_call(
        functools.partial(_attn_kernel, lambda_init=lambda_init, nt=nt,
                          n_pairs=len(pairs)),
        grid_spec=grid_spec,
        out_shape=jax.ShapeDtypeStruct((batch, seq, N_HEADS * V_DIM), BF16),
        compiler_params=_params("arbitrary", "arbitrary"),
        name="diff_attn",
    )(qi, kj, lam_p, subln, qst, k, vt, stat)


CONV_ROWS = 1024
CONV_CHUNK = MXU_DIM


def _conv_kernel(x_ref, g_ref, win_ref, wc_ref, wout_ref, o_ref, tail_ref, z_ref):
    t = CONV_ROWS

    @pl.when(pl.program_id(1) == 0)
    def _():
        tail_ref[...] = jnp.zeros_like(tail_ref)

    x = x_ref[...]
    xn = _rms(x, g_ref[...]).astype(BF16)
    row = lax.broadcasted_iota(jnp.int32, (t, CONV_CHUNK), 0)
    for c in range(D_MODEL // CONV_CHUNK):
        cols = slice(c * CONV_CHUNK, (c + 1) * CONV_CHUNK)
        part = lambda k: jnp.dot(
            xn, win_ref[:, k * D_MODEL + cols.start:k * D_MODEL + cols.stop],
            preferred_element_type=F32)
        gb = part(0)
        u = part(1) * part(2)
        tail = tail_ref[:, cols]
        prev1 = tail[SUBLANES - 1:SUBLANES]
        prev2 = tail[SUBLANES - 2:SUBLANES - 1]
        u1 = jnp.where(row == 0, prev1, pltpu.roll(u, 1, 0))
        u2 = jnp.where(row == 0, prev2,
                       jnp.where(row == 1, prev1, pltpu.roll(u, 2, 0)))
        wc = wc_ref[:, cols]
        y = wc[0:1] * u2 + wc[1:2] * u1 + wc[2:3] * u
        tail_ref[:, cols] = u[t - SUBLANES:, :]
        z_ref[:, cols] = (gb * y).astype(BF16)
    o_ref[...] = x + jnp.dot(z_ref[...], wout_ref[...], preferred_element_type=F32)


def _conv(x, g, w_in, w_conv, w_out, batch, seq):
    nt = seq // CONV_ROWS
    row_spec = pl.BlockSpec((CONV_ROWS, D_MODEL), lambda b, i: (b * nt + i, 0))
    return pl.pallas_call(
        _conv_kernel,
        grid=(batch, nt),
        in_specs=[row_spec, _resident((1, D_MODEL)),
                  _resident((D_MODEL, 3 * D_MODEL)),
                  _resident((CONV_WIDTH, D_MODEL)),
                  _resident((D_MODEL, D_MODEL))],
        out_specs=row_spec,
        out_shape=jax.ShapeDtypeStruct((batch * seq, D_MODEL), F32),
        scratch_shapes=[pltpu.VMEM((SUBLANES, D_MODEL), F32),
                        pltpu.VMEM((CONV_ROWS, D_MODEL), BF16)],
        compiler_params=_params("arbitrary", "arbitrary"),
        name="short_conv",
    )(x, g, w_in, w_conv, w_out)


def kernel(x, ffn1_norm, ffn1_w_gate, ffn1_w_up, ffn1_w_down, mix_norm, attn_w_qkv, attn_lambda_q1, attn_lambda_k1, attn_lambda_q2, attn_lambda_k2, attn_subln, attn_w_out, conv_w_in, conv_w, conv_w_out, ffn2_norm, ffn2_w_gate, ffn2_w_up, ffn2_w_down, final_norm):
    batch, seq, d = x.shape
    depth = ffn1_norm.shape[0]
    bf = lambda w: w.astype(BF16)
    row = lambda v: v.reshape(1, -1)
    h = x.reshape(batch * seq, d)
    for i in range(depth):
        h = _ffn(h, (row(ffn1_norm[i]), bf(ffn1_w_gate[i]), bf(ffn1_w_up[i]),
                     bf(ffn1_w_down[i])))
        j = i // N_MIXERS
        proj = None
        if i % N_MIXERS == 0:
            w = attn_w_qkv[j]
            qst, k, vt, stat = _qkv(h, row(mix_norm[i]), bf(w[:, :d].T),
                                    bf(w[:, d:2 * d]), bf(w[:, 2 * d:].T), batch, seq)
            lam_p = jnp.stack([attn_lambda_q1[j], attn_lambda_k1[j],
                               attn_lambda_q2[j], attn_lambda_k2[j]])
            lambda_init = 0.8 - 0.6 * math.exp(-0.3 * i)
            a = _attn(lam_p, attn_subln[j].reshape(-1, 1), qst, k, vt, stat, batch,
                      seq, lambda_init)
            proj = (a.reshape(batch * seq, d), bf(attn_w_out[j]))
        else:
            h = _conv(h, row(mix_norm[i]), bf(conv_w_in[j]), conv_w[j],
                      bf(conv_w_out[j]), batch, seq)
        h = _ffn(h, (row(ffn2_norm[i]), bf(ffn2_w_gate[i]), bf(ffn2_w_up[i]),
                     bf(ffn2_w_down[i])), proj=proj,
                 final_g=row(final_norm) if i == depth - 1 else None)
    return h.reshape(batch, seq, d)
```

```python
import functools
import math

import jax
import jax.numpy as jnp
from jax import lax
from jax.experimental import pallas as pl
from jax.experimental.pallas import tpu as pltpu

D_MODEL = 1024
N_HEADS = 8
HEAD_DIM = 64
V_DIM = 2 * HEAD_DIM
D_FF = 2816
CONV_WIDTH = 3
EPS = 1e-5
N_MIXERS = 2

LANES = 128
SUBLANES = 8
BF16_ROWS = 16
MXU_DIM = 256
VMEM_LIMIT = 56 * 1024 * 1024

F32 = jnp.float32
BF16 = jnp.bfloat16
NT_DIMS = (((1,), (1,)), ((), ()))
LOG2_E = math.log2(math.e)


def _rms(x, g):
    ms = jnp.mean(x * x, axis=-1, keepdims=True)
    return x * lax.rsqrt(ms + EPS) * g


def _resident(shape):
    return pl.BlockSpec(shape, lambda *_: (0,) * len(shape),
                        pipeline_mode=pl.Buffered(1))


def _params(*sem, flags=None):
    return pltpu.CompilerParams(dimension_semantics=sem,
                                vmem_limit_bytes=VMEM_LIMIT, flags=flags)


FFN_ROWS = 1024
FFN_CHUNK = MXU_DIM


def _ffn_kernel(*refs, has_proj, has_final):
    refs = list(refs)
    x = refs.pop(0)[...]
    if has_proj:
        a_ref, wo_ref = refs.pop(0), refs.pop(0)
        x = x + jnp.dot(a_ref[...], wo_ref[...], preferred_element_type=F32)
    g_ref, wg_ref, wu_ref, wd_ref = refs[:4]
    o_ref, h_ref = refs[-2:]
    xn = _rms(x, g_ref[...]).astype(BF16)
    for c in range(D_FF // FFN_CHUNK):
        sl = slice(c * FFN_CHUNK, (c + 1) * FFN_CHUNK)
        gate = jnp.dot(xn, wg_ref[:, sl], preferred_element_type=F32)
        up = jnp.dot(xn, wu_ref[:, sl], preferred_element_type=F32)
        h_ref[:, sl] = (gate * jax.nn.sigmoid(gate) * up).astype(BF16)
    y = x + 0.5 * jnp.dot(h_ref[...], wd_ref[...], preferred_element_type=F32)
    if has_final:
        y = _rms(y, refs[4][...])
    o_ref[...] = y


def _ffn(x, ffn_w, proj=None, final_g=None):
    rows = x.shape[0]
    row_spec = pl.BlockSpec((FFN_ROWS, D_MODEL), lambda i: (i, 0))
    args, specs = [x], [row_spec]
    if proj is not None:
        args += list(proj)
        specs += [row_spec, _resident((D_MODEL, D_MODEL))]
    args += list(ffn_w)
    specs += [_resident((1, D_MODEL)), _resident((D_MODEL, D_FF)),
              _resident((D_MODEL, D_FF)), _resident((D_FF, D_MODEL))]
    if final_g is not None:
        args.append(final_g)
        specs.append(_resident((1, D_MODEL)))
    return pl.pallas_call(
        functools.partial(_ffn_kernel, has_proj=proj is not None,
                          has_final=final_g is not None),
        grid=(rows // FFN_ROWS,),
        in_specs=specs,
        out_specs=row_spec,
        out_shape=jax.ShapeDtypeStruct((rows, D_MODEL), F32),
        scratch_shapes=[pltpu.VMEM((FFN_ROWS, D_FF), BF16)],
        compiler_params=_params("arbitrary"),
        name="ffn",
    )(*args)


ATT_TILE = 512
STAT_QNORM, STAT_DIAG, STAT_KNORM, STAT_ROWS_USED = 0, 1, 2, 3
BOUND_SLACK = 1.001
SAFE_GAP = 64.0
QKV_TILES = 2
FAST_UNROLL = 24


def _qkv_kernel(x_ref, g_ref, wqt_ref, wk_ref, wvt_ref, qst_ref, k_ref, vt_ref,
                stat_ref):
    t = ATT_TILE
    feat = lax.broadcasted_iota(jnp.int32, (V_DIM, t), 0)
    zero = jnp.zeros((V_DIM, t), BF16)
    col_norm = lambda a: jnp.sqrt(jnp.sum(a * a, axis=0, keepdims=True))
    for s in range(QKV_TILES):
        tok = slice(s * t, (s + 1) * t)
        xn = _rms(x_ref[tok, :], g_ref[...]).astype(BF16)
        qt = lax.dot_general(wqt_ref[...], xn, NT_DIMS, preferred_element_type=F32)
        qt = (qt * (HEAD_DIM ** -0.5 * LOG2_E)).astype(BF16)
        k = jnp.dot(xn, wk_ref[...], preferred_element_type=F32).astype(BF16)
        vt = lax.dot_general(wvt_ref[...], xn, NT_DIMS,
                             preferred_element_type=F32).astype(BF16)
        qf = qt.astype(F32)
        kf = k.astype(F32).T
        for h in range(N_HEADS):
            sl = slice(h * V_DIM, (h + 1) * V_DIM)
            for c in range(2):
                rows = slice(h * V_DIM + c * HEAD_DIM, h * V_DIM + (c + 1) * HEAD_DIM)
                cols = slice(c * t, (c + 1) * t)
                q_c, k_c = qf[rows, :], kf[rows, :]
                stat_ref[0, h, s, STAT_QNORM:STAT_QNORM + 1, cols] = col_norm(q_c)
                stat_ref[0, h, s, STAT_DIAG:STAT_DIAG + 1, cols] = jnp.sum(
                    q_c * k_c, axis=0, keepdims=True)
                stat_ref[0, h, s, STAT_KNORM:STAT_KNORM + 1, cols] = col_norm(k_c)
            stat_ref[0, h, s, STAT_ROWS_USED:, :] = jnp.zeros(
                (SUBLANES - STAT_ROWS_USED, 2 * t), F32)
            qst_ref[0, h, s, :, :t] = jnp.where(feat < HEAD_DIM, qt[sl, :], zero)
            qst_ref[0, h, s, :, t:] = jnp.where(feat >= HEAD_DIM, qt[sl, :], zero)
            k_ref[0, h, tok, :] = k[:, sl]
            vt_ref[0, h, s] = vt[sl, :]


def _qkv(x, g, wqt, wk, wvt, batch, seq):
    nt = seq // ATT_TILE
    steps, rows = nt // QKV_TILES, QKV_TILES * ATT_TILE
    return pl.pallas_call(
        _qkv_kernel,
        grid=(batch, steps),
        in_specs=[pl.BlockSpec((rows, D_MODEL), lambda b, i: (b * steps + i, 0)),
                  _resident((1, D_MODEL)), _resident((D_MODEL, D_MODEL)),
                  _resident((D_MODEL, D_MODEL)), _resident((D_MODEL, D_MODEL))],
        out_specs=[
            pl.BlockSpec((1, N_HEADS, QKV_TILES, V_DIM, 2 * ATT_TILE),
                         lambda b, i: (b, 0, i, 0, 0)),
            pl.BlockSpec((1, N_HEADS, rows, V_DIM), lambda b, i: (b, 0, i, 0)),
            pl.BlockSpec((1, N_HEADS, QKV_TILES, V_DIM, ATT_TILE),
                         lambda b, i: (b, 0, i, 0, 0)),
            pl.BlockSpec((1, N_HEADS, QKV_TILES, SUBLANES, 2 * ATT_TILE),
                         lambda b, i: (b, 0, i, 0, 0)),
        ],
        out_shape=[
            jax.ShapeDtypeStruct((batch, N_HEADS, nt, V_DIM, 2 * ATT_TILE), BF16),
            jax.ShapeDtypeStruct((batch, N_HEADS, seq, V_DIM), BF16),
            jax.ShapeDtypeStruct((batch, N_HEADS, nt, V_DIM, ATT_TILE), BF16),
            jax.ShapeDtypeStruct((batch, N_HEADS, nt, SUBLANES, 2 * ATT_TILE), F32),
        ],
        compiler_params=_params("arbitrary", "arbitrary"),
        name="qkv",
    )(x, g, wqt, wk, wvt)


def _causal_pairs(nt):
    return ([(q, q) for q in range(nt)]
            + [(q, k) for q in range(nt) for k in range(q)])


def _attn_kernel(qi_ref, kj_ref, lam_ref, subln_ref, qst_ref, k_ref, vt_ref, stat_ref,
                 o_ref, s0_ref, smax0_ref, p0_ref, alpha0_ref, s1_ref, smax1_ref,
                 p1_ref, alpha1_ref, m_ref, l_ref, acc_ref, *, lambda_init, nt,
                 n_pairs):
    t = ATT_TILE
    n = n_pairs
    s_ref, smax_ref = (s0_ref, s1_ref), (smax0_ref, smax1_ref)
    p_ref, alpha_ref = (p0_ref, p1_ref), (alpha0_ref, alpha1_ref)

    def scores(i, masked):
        kk = k_ref[0, 0, pl.ds(pl.multiple_of(kj_ref[i] * t, t), t), :]
        s = jnp.dot(kk, qst_ref[0, 0, qi_ref[i]],
                    preferred_element_type=F32)
        if masked:
            key = lax.broadcasted_iota(jnp.int32, (t, 2 * t), 0)
            qry = lax.broadcasted_iota(jnp.int32, (t, 2 * t), 1)
            qry = jnp.where(qry >= t, qry - t, qry)
            s = jnp.where(key <= qry, s, -jnp.inf)
        return s

    def key_sum(p):
        return jnp.sum(p.reshape(p.shape[0] // SUBLANES, SUBLANES, p.shape[1]), axis=0)

    def pv(i, slot):
        return jnp.dot(vt_ref[0, 0, kj_ref[i]], p_ref[slot][...],
                       preferred_element_type=F32)

    halves = (slice(0, t), slice(t, 2 * t))

    def diag_blocks(cols):
        c0, hb = cols.start, t // 2
        return ((hb, slice(c0, c0 + hb), 0), (t, slice(c0 + hb, c0 + t), hb))

    def fast_a(i, slot, masked, cols):
        q = qi_ref[i]
        key0 = pl.multiple_of(kj_ref[i] * t, t)
        for nk, qc, off in (diag_blocks(cols) if masked else ((t, cols, 0),)):
            w = qc.stop - qc.start
            s = jnp.dot(k_ref[0, 0, pl.ds(key0, nk), :], qst_ref[0, 0, q, :, qc],
                        preferred_element_type=F32)
            if masked:
                key = lax.broadcasted_iota(jnp.int32, (nk, w), 0)
                qry = lax.broadcasted_iota(jnp.int32, (nk, w), 1) + off
                s = jnp.where(key <= qry, s, -jnp.inf)
            p = jnp.exp2(s - m_ref[q, :, qc])
            l_ref[q, :, qc] = key_sum(p) if masked else l_ref[q, :, qc] + key_sum(p)
            p_ref[slot][:nk, qc] = p.astype(BF16)
            if masked and nk < t and i == nt - 1:
                p_ref[slot][nk:, qc] = jnp.zeros((t - nk, w), BF16)

    def fast_c(i, slot, cols, first):
        q = qi_ref[i]
        if first:
            for nk, qc, _ in diag_blocks(cols):
                acc_ref[q, :, qc] = jnp.dot(vt_ref[0, 0, kj_ref[i], :, :nk],
                                            p_ref[slot][:nk, qc],
                                            preferred_element_type=F32)
        else:
            acc_ref[q, :, cols] = acc_ref[q, :, cols] + jnp.dot(
                vt_ref[0, 0, kj_ref[i]], p_ref[slot][:, cols],
                preferred_element_type=F32)

    def fast_step(i, slot, masked):
        for cols in halves:
            fast_a(i, slot, masked, cols)
            fast_c(i - 1, 1 - slot, cols, first=masked)

    def slow_a(i, slot, masked):
        s = scores(i, masked)
        s_ref[slot][...] = s
        smax_ref[slot][...] = jnp.max(s, axis=0, keepdims=True)

    def slow_b(i, slot):
        q = qi_ref[i]
        m_prev = m_ref[q]
        m_new = jnp.maximum(m_prev, smax_ref[slot][...])
        alpha = jnp.exp2(m_prev - m_new)
        p = jnp.exp2(s_ref[slot][...] - m_new)
        l_ref[q] = alpha * l_ref[q] + key_sum(p)
        alpha_ref[slot][...] = alpha
        p_ref[slot][...] = p.astype(BF16)
        m_ref[q] = m_new

    def slow_c(i, slot):
        q = qi_ref[i]
        acc_ref[q] = alpha_ref[slot][...] * acc_ref[q] + pv(i, slot)

    def slow_step(i, slot, masked):
        slow_a(i, slot, masked)
        slow_b(i - 1, 1 - slot)
        slow_c(i - 2, slot)

    def multi_steps(step, masked, pairs_per_body, first_pair=0):
        def body(h, carry):
            for u in range(pairs_per_body):
                step(first_pair + pairs_per_body * h + u, u % 2, masked)
            return carry
        return body

    def two_steps(step, masked):
        return multi_steps(step, masked, 2)

    assert nt % 2 == 0 and (n - nt) % FAST_UNROLL == 0 and FAST_UNROLL % 2 == 0

    def fast_path():
        for cols in halves:
            fast_a(0, 0, True, cols)
        for i in range(1, nt):
            fast_step(i, i % 2, True)
        acc_ref[nt - 1] = jnp.zeros(acc_ref.shape[1:], F32)
        lax.fori_loop(0, (n - nt) // FAST_UNROLL,
                      multi_steps(fast_step, False, FAST_UNROLL, first_pair=nt), 0)
        for cols in halves:
            fast_c(n - 1, 1, cols, first=False)

    def slow_path():
        m_ref[...] = jnp.full_like(m_ref, -jnp.inf)
        l_ref[...] = jnp.zeros_like(l_ref)
        acc_ref[...] = jnp.zeros_like(acc_ref)
        slow_a(0, 0, True)
        slow_a(1, 1, True)
        slow_b(0, 0)
        lax.fori_loop(1, nt // 2, two_steps(slow_step, True), 0)
        lax.fori_loop(nt // 2, n // 2, two_steps(slow_step, False), 0)
        slow_b(n - 1, 1)
        slow_c(n - 2, 0)
        slow_c(n - 1, 1)

    stat = stat_ref[0, 0]
    k_norm = jnp.max(stat[:, STAT_KNORM:STAT_KNORM + 1, :], axis=0)
    lane = lax.broadcasted_iota(jnp.int32, (1, 2 * t), 1)
    neg = jnp.full_like(k_norm, -jnp.inf)
    k_max = jnp.where(
        lane < t,
        jnp.max(jnp.where(lane < t, k_norm, neg), axis=1, keepdims=True),
        jnp.max(jnp.where(lane >= t, k_norm, neg), axis=1, keepdims=True))
    shift = stat[:, STAT_QNORM:STAT_QNORM + 1, :] * (k_max * BOUND_SLACK)
    gap = shift - stat[:, STAT_DIAG:STAT_DIAG + 1, :]
    safe = jnp.max(jnp.where(gap <= SAFE_GAP, 0.0, 1.0)) == 0.0
    m_ref[...] = shift
    lax.cond(safe, fast_path, slow_path)

    def finalize(q, carry):
        lam_p = lam_ref[...]
        lam = (jnp.exp(jnp.sum(lam_p[0:1] * lam_p[1:2], axis=1, keepdims=True))
               - jnp.exp(jnp.sum(lam_p[2:3] * lam_p[3:4], axis=1, keepdims=True))
               + lambda_init)
        acc = acc_ref[q]
        inv_sum = 1.0 / jnp.sum(l_ref[q], axis=0, keepdims=True)
        d = (acc[:, :t] * inv_sum[:, :t]
             - acc[:, t:] * (lam * inv_sum[:, t:]))
        ms = jnp.mean(d * d, axis=0, keepdims=True)
        o = d * lax.rsqrt(ms + EPS) * (subln_ref[...] * (1.0 - lambda_init))
        o_ref[0, pl.ds(pl.multiple_of(q * t, t), t), :] = o.T.astype(BF16)
        return carry

    lax.fori_loop(0, nt, finalize, 0, unroll=4)


def _attn(lam_p, subln, qst, k, vt, stat, batch, seq, lambda_init):
    nt = seq // ATT_TILE
    pairs = _causal_pairs(nt)
    qi = jnp.array([p[0] for p in pairs], jnp.int32)
    kj = jnp.array([p[1] for p in pairs], jnp.int32)
    grid_spec = pltpu.PrefetchScalarGridSpec(
        num_scalar_prefetch=2,
        grid=(batch, N_HEADS),
        in_specs=[
            _resident((4, HEAD_DIM)), _resident((V_DIM, 1)),
            pl.BlockSpec((1, 1, nt, V_DIM, 2 * ATT_TILE),
                         lambda b, h, *_: (b, h, 0, 0, 0)),
            pl.BlockSpec((1, 1, seq, V_DIM), lambda b, h, *_: (b, h, 0, 0)),
            pl.BlockSpec((1, 1, nt, V_DIM, ATT_TILE),
                         lambda b, h, *_: (b, h, 0, 0, 0)),
            pl.BlockSpec((1, 1, nt, SUBLANES, 2 * ATT_TILE),
                         lambda b, h, *_: (b, h, 0, 0, 0)),
        ],
        out_specs=pl.BlockSpec((1, seq, V_DIM), lambda b, h, *_: (b, 0, h)),
        scratch_shapes=2 * [pltpu.VMEM((ATT_TILE, 2 * ATT_TILE), F32),
                            pltpu.VMEM((1, 2 * ATT_TILE), F32),
                            pltpu.VMEM((ATT_TILE, 2 * ATT_TILE), BF16),
                            pltpu.VMEM((1, 2 * ATT_TILE), F32)]
        + [pltpu.VMEM((nt, 1, 2 * ATT_TILE), F32),
           pltpu.VMEM((nt, SUBLANES, 2 * ATT_TILE), F32),
           pltpu.VMEM((nt, V_DIM, 2 * ATT_TILE), F32)],
    )
    return pl.pallas_call(
        functools.partial(_attn_kernel, lambda_init=lambda_init, nt=nt,
                          n_pairs=len(pairs)),
        grid_spec=grid_spec,
        out_shape=jax.ShapeDtypeStruct((batch, seq, N_HEADS * V_DIM), BF16),
        compiler_params=_params("arbitrary", "arbitrary"),
        name="diff_attn",
    )(qi, kj, lam_p, subln, qst, k, vt, stat)


CONV_ROWS = 1024
CONV_CHUNK = MXU_DIM


def _conv_kernel(x_ref, g_ref, win_ref, wc_ref, wout_ref, o_ref, tail_ref, z_ref):
    t = CONV_ROWS

    @pl.when(pl.program_id(1) == 0)
    def _():
        tail_ref[...] = jnp.zeros_like(tail_ref)

    x = x_ref[...]
    xn = _rms(x, g_ref[...]).astype(BF16)
    row = lax.broadcasted_iota(jnp.int32, (t, CONV_CHUNK), 0)
    for c in range(D_MODEL // CONV_CHUNK):
        cols = slice(c * CONV_CHUNK, (c + 1) * CONV_CHUNK)
        part = lambda k: jnp.dot(
            xn, win_ref[:, k * D_MODEL + cols.start:k * D_MODEL + cols.stop],
            preferred_element_type=F32)
        gb = part(0)
        u = part(1) * part(2)
        tail = tail_ref[:, cols]
        prev1 = tail[SUBLANES - 1:SUBLANES]
        prev2 = tail[SUBLANES - 2:SUBLANES - 1]
        u1 = jnp.where(row == 0, prev1, pltpu.roll(u, 1, 0))
        u2 = jnp.where(row == 0, prev2,
                       jnp.where(row == 1, prev1, pltpu.roll(u, 2, 0)))
        wc = wc_ref[:, cols]
        y = wc[0:1] * u2 + wc[1:2] * u1 + wc[2:3] * u
        tail_ref[:, cols] = u[t - SUBLANES:, :]
        z_ref[:, cols] = (gb * y).astype(BF16)
    o_ref[...] = x + jnp.dot(z_ref[...], wout_ref[...], preferred_element_type=F32)


def _conv(x, g, w_in, w_conv, w_out, batch, seq):
    nt = seq // CONV_ROWS
    row_spec = pl.BlockSpec((CONV_ROWS, D_MODEL), lambda b, i: (b * nt + i, 0))
    return pl.pallas_call(
        _conv_kernel,
        grid=(batch, nt),
        in_specs=[row_spec, _resident((1, D_MODEL)),
                  _resident((D_MODEL, 3 * D_MODEL)),
                  _resident((CONV_WIDTH, D_MODEL)),
                  _resident((D_MODEL, D_MODEL))],
        out_specs=row_spec,
        out_shape=jax.ShapeDtypeStruct((batch * seq, D_MODEL), F32),
        scratch_shapes=[pltpu.VMEM((SUBLANES, D_MODEL), F32),
                        pltpu.VMEM((CONV_ROWS, D_MODEL), BF16)],
        compiler_params=_params("arbitrary", "arbitrary"),
        name="short_conv",
    )(x, g, w_in, w_conv, w_out)


def kernel(x, ffn1_norm, ffn1_w_gate, ffn1_w_up, ffn1_w_down, mix_norm, attn_w_qkv, attn_lambda_q1, attn_lambda_k1, attn_lambda_q2, attn_lambda_k2, attn_subln, attn_w_out, conv_w_in, conv_w, conv_w_out, ffn2_norm, ffn2_w_gate, ffn2_w_up, ffn2_w_down, final_norm):
    batch, seq, d = x.shape
    depth = ffn1_norm.shape[0]
    bf = lambda w: w.astype(BF16)
    row = lambda v: v.reshape(1, -1)
    h = x.reshape(batch * seq, d)
    for i in range(depth):
        h = _ffn(h, (row(ffn1_norm[i]), bf(ffn1_w_gate[i]), bf(ffn1_w_up[i]),
                     bf(ffn1_w_down[i])))
        j = i // N_MIXERS
        proj = None
        if i % N_MIXERS == 0:
            w = attn_w_qkv[j]
            qst, k, vt, stat = _qkv(h, row(mix_norm[i]), bf(w[:, :d].T),
                                    bf(w[:, d:2 * d]), bf(w[:, 2 * d:].T), batch, seq)
            lam_p = jnp.stack([attn_lambda_q1[j], attn_lambda_k1[j],
                               attn_lambda_q2[j], attn_lambda_k2[j]])
            lambda_init = 0.8 - 0.6 * math.exp(-0.3 * i)
            a = _attn(lam_p, attn_subln[j].reshape(-1, 1), qst, k, vt, stat, batch,
                      seq, lambda_init)
            proj = (a.reshape(batch * seq, d), bf(attn_w_out[j]))
        else:
            h = _conv(h, row(mix_norm[i]), bf(conv_w_in[j]), conv_w[j],
                      bf(conv_w_out[j]), batch, seq)
        h = _ffn(h, (row(ffn2_norm[i]), bf(ffn2_w_gate[i]), bf(ffn2_w_up[i]),
                     bf(ffn2_w_down[i])), proj=proj,
                 final_g=row(final_norm) if i == depth - 1 else None)
    return h.reshape(batch, seq, d)
```

```python
import functools
import math

import jax
import jax.numpy as jnp
from jax import lax
from jax.experimental import pallas as pl
from jax.experimental.pallas import tpu as pltpu

D_MODEL = 1024
N_HEADS = 8
HEAD_DIM = 64
V_DIM = 2 * HEAD_DIM
D_FF = 2816
CONV_WIDTH = 3
EPS = 1e-5
N_MIXERS = 2

LANES = 128
SUBLANES = 8
BF16_ROWS = 16
MXU_DIM = 256
VMEM_LIMIT = 56 * 1024 * 1024

F32 = jnp.float32
BF16 = jnp.bfloat16
NT_DIMS = (((1,), (1,)), ((), ()))
LOG2_E = math.log2(math.e)


def _rms(x, g):
    ms = jnp.mean(x * x, axis=-1, keepdims=True)
    return x * lax.rsqrt(ms + EPS) * g


def _resident(shape):
    return pl.BlockSpec(shape, lambda *_: (0,) * len(shape),
                        pipeline_mode=pl.Buffered(1))


def _params(*sem, flags=None):
    return pltpu.CompilerParams(dimension_semantics=sem,
                                vmem_limit_bytes=VMEM_LIMIT, flags=flags)


FFN_ROWS = 1024
FFN_CHUNK = MXU_DIM


def _ffn_kernel(*refs, has_proj, has_final):
    refs = list(refs)
    x = refs.pop(0)[...]
    if has_proj:
        a_ref, wo_ref = refs.pop(0), refs.pop(0)
        x = x + jnp.dot(a_ref[...], wo_ref[...], preferred_element_type=F32)
    g_ref, wgu_ref, wd_ref = refs[:3]
    o_ref, h_ref = refs[-2:]
    xn = _rms(x, g_ref[...]).astype(BF16)
    for c in range(D_FF // FFN_CHUNK):
        sl = slice(c * FFN_CHUNK, (c + 1) * FFN_CHUNK)
        gu = jnp.dot(xn, wgu_ref[:, 2 * c * FFN_CHUNK:2 * (c + 1) * FFN_CHUNK],
                     preferred_element_type=F32)
        gate, up = gu[:, :FFN_CHUNK], gu[:, FFN_CHUNK:]
        h_ref[:, sl] = (gate * jax.nn.sigmoid(gate) * up).astype(BF16)
    y = x + 0.5 * jnp.dot(h_ref[...], wd_ref[...], preferred_element_type=F32)
    if has_final:
        y = _rms(y, refs[3][...])
    o_ref[...] = y


def _ffn(x, ffn_w, proj=None, final_g=None):
    rows = x.shape[0]
    row_spec = pl.BlockSpec((FFN_ROWS, D_MODEL), lambda i: (i, 0))
    args, specs = [x], [row_spec]
    if proj is not None:
        args += list(proj)
        specs += [row_spec, _resident((D_MODEL, D_MODEL))]
    args += list(ffn_w)
    specs += [_resident((1, D_MODEL)), _resident((D_MODEL, 2 * D_FF)),
              _resident((D_FF, D_MODEL))]
    if final_g is not None:
        args.append(final_g)
        specs.append(_resident((1, D_MODEL)))
    return pl.pallas_call(
        functools.partial(_ffn_kernel, has_proj=proj is not None,
                          has_final=final_g is not None),
        grid=(rows // FFN_ROWS,),
        in_specs=specs,
        out_specs=row_spec,
        out_shape=jax.ShapeDtypeStruct((rows, D_MODEL), F32),
        scratch_shapes=[pltpu.VMEM((FFN_ROWS, D_FF), BF16)],
        compiler_params=_params("arbitrary"),
        name="ffn",
    )(*args)


ATT_TILE = 512
STAT_QNORM, STAT_DIAG, STAT_KNORM, STAT_ROWS_USED = 0, 1, 2, 3
BOUND_SLACK = 1.001
SAFE_GAP = 64.0
QKV_TILES = 2
FAST_UNROLL = 24


def _qkv_kernel(x_ref, g_ref, wqt_ref, wk_ref, wvt_ref, qst_ref, k_ref, vt_ref,
                stat_ref):
    t = ATT_TILE
    feat = lax.broadcasted_iota(jnp.int32, (V_DIM, t), 0)
    zero = jnp.zeros((V_DIM, t), BF16)
    col_norm = lambda a: jnp.sqrt(jnp.sum(a * a, axis=0, keepdims=True))
    for s in range(QKV_TILES):
        tok = slice(s * t, (s + 1) * t)
        xn = _rms(x_ref[tok, :], g_ref[...]).astype(BF16)
        qt = lax.dot_general(wqt_ref[...], xn, NT_DIMS, preferred_element_type=F32)
        qt = (qt * (HEAD_DIM ** -0.5 * LOG2_E)).astype(BF16)
        k = jnp.dot(xn, wk_ref[...], preferred_element_type=F32).astype(BF16)
        vt = lax.dot_general(wvt_ref[...], xn, NT_DIMS,
                             preferred_element_type=F32).astype(BF16)
        qf = qt.astype(F32)
        kf = k.astype(F32).T
        for h in range(N_HEADS):
            sl = slice(h * V_DIM, (h + 1) * V_DIM)
            for c in range(2):
                rows = slice(h * V_DIM + c * HEAD_DIM, h * V_DIM + (c + 1) * HEAD_DIM)
                cols = slice(c * t, (c + 1) * t)
                q_c, k_c = qf[rows, :], kf[rows, :]
                stat_ref[0, h, s, STAT_QNORM:STAT_QNORM + 1, cols] = col_norm(q_c)
                stat_ref[0, h, s, STAT_DIAG:STAT_DIAG + 1, cols] = jnp.sum(
                    q_c * k_c, axis=0, keepdims=True)
                stat_ref[0, h, s, STAT_KNORM:STAT_KNORM + 1, cols] = col_norm(k_c)
            stat_ref[0, h, s, STAT_ROWS_USED:, :] = jnp.zeros(
                (SUBLANES - STAT_ROWS_USED, 2 * t), F32)
            qst_ref[0, h, s, :, :t] = jnp.where(feat < HEAD_DIM, qt[sl, :], zero)
            qst_ref[0, h, s, :, t:] = jnp.where(feat >= HEAD_DIM, qt[sl, :], zero)
            k_ref[0, h, tok, :] = k[:, sl]
            vt_ref[0, h, s] = vt[sl, :]


def _qkv(x, g, wqt, wk, wvt, batch, seq):
    nt = seq // ATT_TILE
    steps, rows = nt // QKV_TILES, QKV_TILES * ATT_TILE
    return pl.pallas_call(
        _qkv_kernel,
        grid=(batch, steps),
        in_specs=[pl.BlockSpec((rows, D_MODEL), lambda b, i: (b * steps + i, 0)),
                  _resident((1, D_MODEL)), _resident((D_MODEL, D_MODEL)),
                  _resident((D_MODEL, D_MODEL)), _resident((D_MODEL, D_MODEL))],
        out_specs=[
            pl.BlockSpec((1, N_HEADS, QKV_TILES, V_DIM, 2 * ATT_TILE),
                         lambda b, i: (b, 0, i, 0, 0)),
            pl.BlockSpec((1, N_HEADS, rows, V_DIM), lambda b, i: (b, 0, i, 0)),
            pl.BlockSpec((1, N_HEADS, QKV_TILES, V_DIM, ATT_TILE),
                         lambda b, i: (b, 0, i, 0, 0)),
            pl.BlockSpec((1, N_HEADS, QKV_TILES, SUBLANES, 2 * ATT_TILE),
                         lambda b, i: (b, 0, i, 0, 0)),
        ],
        out_shape=[
            jax.ShapeDtypeStruct((batch, N_HEADS, nt, V_DIM, 2 * ATT_TILE), BF16),
            jax.ShapeDtypeStruct((batch, N_HEADS, seq, V_DIM), BF16),
            jax.ShapeDtypeStruct((batch, N_HEADS, nt, V_DIM, ATT_TILE), BF16),
            jax.ShapeDtypeStruct((batch, N_HEADS, nt, SUBLANES, 2 * ATT_TILE), F32),
        ],
        compiler_params=_params("arbitrary", "arbitrary"),
        name="qkv",
    )(x, g, wqt, wk, wvt)


def _causal_pairs(nt):
    return ([(q, q) for q in range(nt)]
            + [(q, k) for q in range(nt) for k in range(q)])


def _attn_kernel(qi_ref, kj_ref, lam_ref, subln_ref, qst_ref, k_ref, vt_ref, stat_ref,
                 o_ref, s0_ref, smax0_ref, p0_ref, alpha0_ref, s1_ref, smax1_ref,
                 p1_ref, alpha1_ref, m_ref, l_ref, acc_ref, *, lambda_init, nt,
                 n_pairs):
    t = ATT_TILE
    n = n_pairs
    s_ref, smax_ref = (s0_ref, s1_ref), (smax0_ref, smax1_ref)
    p_ref, alpha_ref = (p0_ref, p1_ref), (alpha0_ref, alpha1_ref)

    def scores(i, masked):
        kk = k_ref[0, 0, pl.ds(pl.multiple_of(kj_ref[i] * t, t), t), :]
        s = jnp.dot(kk, qst_ref[0, 0, qi_ref[i]],
                    preferred_element_type=F32)
        if masked:
            key = lax.broadcasted_iota(jnp.int32, (t, 2 * t), 0)
            qry = lax.broadcasted_iota(jnp.int32, (t, 2 * t), 1)
            qry = jnp.where(qry >= t, qry - t, qry)
            s = jnp.where(key <= qry, s, -jnp.inf)
        return s

    def key_sum(p):
        return jnp.sum(p.reshape(p.shape[0] // SUBLANES, SUBLANES, p.shape[1]), axis=0)

    def pv(i, slot):
        return jnp.dot(vt_ref[0, 0, kj_ref[i]], p_ref[slot][...],
                       preferred_element_type=F32)

    halves = (slice(0, t), slice(t, 2 * t))

    def diag_blocks(cols):
        c0, hb = cols.start, t // 2
        return ((hb, slice(c0, c0 + hb), 0), (t, slice(c0 + hb, c0 + t), hb))

    def fast_a(i, slot, masked, cols):
        q = qi_ref[i]
        key0 = pl.multiple_of(kj_ref[i] * t, t)
        for nk, qc, off in (diag_blocks(cols) if masked else ((t, cols, 0),)):
            w = qc.stop - qc.start
            s = jnp.dot(k_ref[0, 0, pl.ds(key0, nk), :], qst_ref[0, 0, q, :, qc],
                        preferred_element_type=F32)
            if masked:
                key = lax.broadcasted_iota(jnp.int32, (nk, w), 0)
                qry = lax.broadcasted_iota(jnp.int32, (nk, w), 1) + off
                s = jnp.where(key <= qry, s, -jnp.inf)
            p = jnp.exp2(s - m_ref[q, :, qc])
            l_ref[q, :, qc] = key_sum(p) if masked else l_ref[q, :, qc] + key_sum(p)
            p_ref[slot][:nk, qc] = p.astype(BF16)
            if masked and nk < t and i == nt - 1:
                p_ref[slot][nk:, qc] = jnp.zeros((t - nk, w), BF16)

    def fast_c(i, slot, cols, first):
        q = qi_ref[i]
        if first:
            for nk, qc, _ in diag_blocks(cols):
                acc_ref[q, :, qc] = jnp.dot(vt_ref[0, 0, kj_ref[i], :, :nk],
                                            p_ref[slot][:nk, qc],
                                            preferred_element_type=F32)
        else:
            acc_ref[q, :, cols] = acc_ref[q, :, cols] + jnp.dot(
                vt_ref[0, 0, kj_ref[i]], p_ref[slot][:, cols],
                preferred_element_type=F32)

    def fast_step(i, slot, masked):
        for cols in halves:
            fast_a(i, slot, masked, cols)
            fast_c(i - 1, 1 - slot, cols, first=masked)

    def slow_a(i, slot, masked):
        s = scores(i, masked)
        s_ref[slot][...] = s
        smax_ref[slot][...] = jnp.max(s, axis=0, keepdims=True)

    def slow_b(i, slot):
        q = qi_ref[i]
        m_prev = m_ref[q]
        m_new = jnp.maximum(m_prev, smax_ref[slot][...])
        alpha = jnp.exp2(m_prev - m_new)
        p = jnp.exp2(s_ref[slot][...] - m_new)
        l_ref[q] = alpha * l_ref[q] + key_sum(p)
        alpha_ref[slot][...] = alpha
        p_ref[slot][...] = p.astype(BF16)
        m_ref[q] = m_new

    def slow_c(i, slot):
        q = qi_ref[i]
        acc_ref[q] = alpha_ref[slot][...] * acc_ref[q] + pv(i, slot)

    def slow_step(i, slot, masked):
        slow_a(i, slot, masked)
        slow_b(i - 1, 1 - slot)
        slow_c(i - 2, slot)

    def multi_steps(step, masked, pairs_per_body, first_pair=0):
        def body(h, carry):
            for u in range(pairs_per_body):
                step(first_pair + pairs_per_body * h + u, u % 2, masked)
            return carry
        return body

    def two_steps(step, masked):
        return multi_steps(step, masked, 2)

    assert nt % 2 == 0 and (n - nt) % FAST_UNROLL == 0 and FAST_UNROLL % 2 == 0

    def fast_path():
        for cols in halves:
            fast_a(0, 0, True, cols)
        for i in range(1, nt):
            fast_step(i, i % 2, True)
        acc_ref[nt - 1] = jnp.zeros(acc_ref.shape[1:], F32)
        lax.fori_loop(0, (n - nt) // FAST_UNROLL,
                      multi_steps(fast_step, False, FAST_UNROLL, first_pair=nt), 0)
        for cols in halves:
            fast_c(n - 1, 1, cols, first=False)

    def slow_path():
        m_ref[...] = jnp.full_like(m_ref, -jnp.inf)
        l_ref[...] = jnp.zeros_like(l_ref)
        acc_ref[...] = jnp.zeros_like(acc_ref)
        slow_a(0, 0, True)
        slow_a(1, 1, True)
        slow_b(0, 0)
        lax.fori_loop(1, nt // 2, two_steps(slow_step, True), 0)
        lax.fori_loop(nt // 2, n // 2, two_steps(slow_step, False), 0)
        slow_b(n - 1, 1)
        slow_c(n - 2, 0)
        slow_c(n - 1, 1)

    stat = stat_ref[0, 0]
    k_norm = jnp.max(stat[:, STAT_KNORM:STAT_KNORM + 1, :], axis=0)
    lane = lax.broadcasted_iota(jnp.int32, (1, 2 * t), 1)
    neg = jnp.full_like(k_norm, -jnp.inf)
    k_max = jnp.where(
        lane < t,
        jnp.max(jnp.where(lane < t, k_norm, neg), axis=1, keepdims=True),
        jnp.max(jnp.where(lane >= t, k_norm, neg), axis=1, keepdims=True))
    shift = stat[:, STAT_QNORM:STAT_QNORM + 1, :] * (k_max * BOUND_SLACK)
    gap = shift - stat[:, STAT_DIAG:STAT_DIAG + 1, :]
    safe = jnp.max(jnp.where(gap <= SAFE_GAP, 0.0, 1.0)) == 0.0
    m_ref[...] = shift
    lax.cond(safe, fast_path, slow_path)

    def finalize(q, carry):
        lam_p = lam_ref[...]
        lam = (jnp.exp(jnp.sum(lam_p[0:1] * lam_p[1:2], axis=1, keepdims=True))
               - jnp.exp(jnp.sum(lam_p[2:3] * lam_p[3:4], axis=1, keepdims=True))
               + lambda_init)
        acc = acc_ref[q]
        inv_sum = 1.0 / jnp.sum(l_ref[q], axis=0, keepdims=True)
        d = (acc[:, :t] * inv_sum[:, :t]
             - acc[:, t:] * (lam * inv_sum[:, t:]))
        ms = jnp.mean(d * d, axis=0, keepdims=True)
        o = d * lax.rsqrt(ms + EPS) * (subln_ref[...] * (1.0 - lambda_init))
        o_ref[0, pl.ds(pl.multiple_of(q * t, t), t), :] = o.T.astype(BF16)
        return carry

    lax.fori_loop(0, nt, finalize, 0, unroll=4)


def _attn(lam_p, subln, qst, k, vt, stat, batch, seq, lambda_init):
    nt = seq // ATT_TILE
    pairs = _causal_pairs(nt)
    qi = jnp.array([p[0] for p in pairs], jnp.int32)
    kj = jnp.array([p[1] for p in pairs], jnp.int32)
    grid_spec = pltpu.PrefetchScalarGridSpec(
        num_scalar_prefetch=2,
        grid=(batch, N_HEADS),
        in_specs=[
            _resident((4, HEAD_DIM)), _resident((V_DIM, 1)),
            pl.BlockSpec((1, 1, nt, V_DIM, 2 * ATT_TILE),
                         lambda b, h, *_: (b, h, 0, 0, 0)),
            pl.BlockSpec((1, 1, seq, V_DIM), lambda b, h, *_: (b, h, 0, 0)),
            pl.BlockSpec((1, 1, nt, V_DIM, ATT_TILE),
                         lambda b, h, *_: (b, h, 0, 0, 0)),
            pl.BlockSpec((1, 1, nt, SUBLANES, 2 * ATT_TILE),
                         lambda b, h, *_: (b, h, 0, 0, 0)),
        ],
        out_specs=pl.BlockSpec((1, seq, V_DIM), lambda b, h, *_: (b, 0, h)),
        scratch_shapes=2 * [pltpu.VMEM((ATT_TILE, 2 * ATT_TILE), F32),
                            pltpu.VMEM((1, 2 * ATT_TILE), F32),
                            pltpu.VMEM((ATT_TILE, 2 * ATT_TILE), BF16),
                            pltpu.VMEM((1, 2 * ATT_TILE), F32)]
        + [pltpu.VMEM((nt, 1, 2 * ATT_TILE), F32),
           pltpu.VMEM((nt, SUBLANES, 2 * ATT_TILE), F32),
           pltpu.VMEM((nt, V_DIM, 2 * ATT_TILE), F32)],
    )
    return pl.pallas_call(
        functools.partial(_attn_kernel, lambda_init=lambda_init, nt=nt,
                          n_pairs=len(pairs)),
        grid_spec=grid_spec,
        out_shape=jax.ShapeDtypeStruct((batch, seq, N_HEADS * V_DIM), BF16),
        compiler_params=_params("arbitrary", "arbitrary"),
        name="diff_attn",
    )(qi, kj, lam_p, subln, qst, k, vt, stat)


CONV_ROWS = 1024
CONV_CHUNK = MXU_DIM


def _conv_kernel(x_ref, g_ref, win_ref, wc_ref, wout_ref, o_ref, tail_ref, z_ref):
    t = CONV_ROWS

    @pl.when(pl.program_id(1) == 0)
    def _():
        tail_ref[...] = jnp.zeros_like(tail_ref)

    x = x_ref[...]
    xn = _rms(x, g_ref[...]).astype(BF16)
    row = lax.broadcasted_iota(jnp.int32, (t, CONV_CHUNK), 0)
    for c in range(D_MODEL // CONV_CHUNK):
        cols = slice(c * CONV_CHUNK, (c + 1) * CONV_CHUNK)
        part = lambda k: jnp.dot(
            xn, win_ref[:, k * D_MODEL + cols.start:k * D_MODEL + cols.stop],
            preferred_element_type=F32)
        gb = part(0)
        u = part(1) * part(2)
        tail = tail_ref[:, cols]
        prev1 = tail[SUBLANES - 1:SUBLANES]
        prev2 = tail[SUBLANES - 2:SUBLANES - 1]
        u1 = jnp.where(row == 0, prev1, pltpu.roll(u, 1, 0))
        u2 = jnp.where(row == 0, prev2,
                       jnp.where(row == 1, prev1, pltpu.roll(u, 2, 0)))
        wc = wc_ref[:, cols]
        y = wc[0:1] * u2 + wc[1:2] * u1 + wc[2:3] * u
        tail_ref[:, cols] = u[t - SUBLANES:, :]
        z_ref[:, cols] = (gb * y).astype(BF16)
    o_ref[...] = x + jnp.dot(z_ref[...], wout_ref[...], preferred_element_type=F32)


def _conv(x, g, w_in, w_conv, w_out, batch, seq):
    nt = seq // CONV_ROWS
    row_spec = pl.BlockSpec((CONV_ROWS, D_MODEL), lambda b, i: (b * nt + i, 0))
    return pl.pallas_call(
        _conv_kernel,
        grid=(batch, nt),
        in_specs=[row_spec, _resident((1, D_MODEL)),
                  _resident((D_MODEL, 3 * D_MODEL)),
                  _resident((CONV_WIDTH, D_MODEL)),
                  _resident((D_MODEL, D_MODEL))],
        out_specs=row_spec,
        out_shape=jax.ShapeDtypeStruct((batch * seq, D_MODEL), F32),
        scratch_shapes=[pltpu.VMEM((SUBLANES, D_MODEL), F32),
                        pltpu.VMEM((CONV_ROWS, D_MODEL), BF16)],
        compiler_params=_params("arbitrary", "arbitrary"),
        name="short_conv",
    )(x, g, w_in, w_conv, w_out)


def kernel(x, ffn1_norm, ffn1_w_gate, ffn1_w_up, ffn1_w_down, mix_norm, attn_w_qkv, attn_lambda_q1, attn_lambda_k1, attn_lambda_q2, attn_lambda_k2, attn_subln, attn_w_out, conv_w_in, conv_w, conv_w_out, ffn2_norm, ffn2_w_gate, ffn2_w_up, ffn2_w_down, final_norm):
    batch, seq, d = x.shape
    depth = ffn1_norm.shape[0]
    bf = lambda w: w.astype(BF16)
    row = lambda v: v.reshape(1, -1)
    h = x.reshape(batch * seq, d)

    def gate_up(wg, wu):
        parts = [w.reshape(d, -1, FFN_CHUNK) for w in (wg, wu)]
        return bf(jnp.stack(parts, axis=2).reshape(d, -1))

    for i in range(depth):
        h = _ffn(h, (row(ffn1_norm[i]), gate_up(ffn1_w_gate[i], ffn1_w_up[i]),
                     bf(ffn1_w_down[i])))
        j = i // N_MIXERS
        proj = None
        if i % N_MIXERS == 0:
            w = attn_w_qkv[j]
            qst, k, vt, stat = _qkv(h, row(mix_norm[i]), bf(w[:, :d].T),
                                    bf(w[:, d:2 * d]), bf(w[:, 2 * d:].T), batch, seq)
            lam_p = jnp.stack([attn_lambda_q1[j], attn_lambda_k1[j],
                               attn_lambda_q2[j], attn_lambda_k2[j]])
            lambda_init = 0.8 - 0.6 * math.exp(-0.3 * i)
            a = _attn(lam_p, attn_subln[j].reshape(-1, 1), qst, k, vt, stat, batch,
                      seq, lambda_init)
            proj = (a.reshape(batch * seq, d), bf(attn_w_out[j]))
        else:
            h = _conv(h, row(mix_norm[i]), bf(conv_w_in[j]), conv_w[j],
                      bf(conv_w_out[j]), batch, seq)
        h = _ffn(h, (row(ffn2_norm[i]), gate_up(ffn2_w_gate[i], ffn2_w_up[i]),
                     bf(ffn2_w_down[i])), proj=proj,
                 final_g=row(final_norm) if i == depth - 1 else None)
    return h.reshape(batch, seq, d)
```
